```python
import math
import jax, jax.numpy as jnp
from jax import lax
import numpy as np

D_MODEL = 2048
BATCH = 8
SEQ = 2048
DEPTH = 2

N_A_LAYERS = DEPTH // 2
N_B_LAYERS = DEPTH - N_A_LAYERS
GLA_HEADS = 4
GLA_DK = D_MODEL // 2
GLA_DV = D_MODEL
GLA_HK = GLA_DK // GLA_HEADS
GLA_HV = GLA_DV // GLA_HEADS
GATE_RANK = 16
GATE_TAU = 16.0
GLA_CHUNK = 64
GLA_IN = 2 * GLA_DK + 2 * GLA_DV + GATE_RANK
DIFF_HEADS = 8
DIFF_HD = D_MODEL // DIFF_HEADS // 2
Q_BLOCK = 128
LAMBDA_INIT_STD = 0.1
REL_BUCKETS = 32
REL_MAX_EXACT = REL_BUCKETS // 2
REL_MAX_DIST = 128
D_FF = ((8 * D_MODEL // 3 + 255) // 256) * 256
EPS = 1e-6

kernel_name = "yoco_gla_diffattn_hybrid"


def rmsnorm(x, g):
    xf = x.astype(jnp.float32)
    y = xf * lax.rsqrt(jnp.mean(xf * xf, axis=-1, keepdims=True) + EPS)
    return (y * g.astype(jnp.float32)).astype(x.dtype)


def swiglu(h, w_gate_up, w_down):
    gate, up = jnp.split(h @ w_gate_up, 2, axis=-1)
    return (jax.nn.silu(gate) * up) @ w_down


def _to_chunks(t, n_heads, head_dim):
    b, s, _ = t.shape
    return t.reshape(b, s // GLA_CHUNK, GLA_CHUNK, n_heads, head_dim).transpose(0, 3, 1, 2, 4)


def gla_mixer(h, w_in, w_fgate, b_fgate, g_norm, w_out):
    b, s, _ = h.shape
    f32 = jnp.float32
    proj = h @ w_in
    q, k, v, r, g_lr = jnp.split(
        proj, [GLA_DK, 2 * GLA_DK, 2 * GLA_DK + GLA_DV, 2 * GLA_DK + 2 * GLA_DV], axis=-1)
    log_a = jax.nn.log_sigmoid((g_lr @ w_fgate + b_fgate).astype(f32)) / GATE_TAU
    q = _to_chunks(q.astype(f32), GLA_HEADS, GLA_HK) * (GLA_HK ** -0.5)
    k = _to_chunks(k.astype(f32), GLA_HEADS, GLA_HK)
    v = _to_chunks(v.astype(f32), GLA_HEADS, GLA_HV)
    bcum = jnp.cumsum(_to_chunks(log_a, GLA_HEADS, GLA_HK), axis=-2)
    b_last = bcum[..., -1:, :]
    q_dec = q * jnp.exp(bcum)
    k_inv = k * jnp.exp(-bcum)
    k_end = k * jnp.exp(b_last - bcum)
    causal = jnp.tril(jnp.ones((GLA_CHUNK, GLA_CHUNK), dtype=bool))
    att = jnp.where(causal, jnp.einsum('bhncd,bhnjd->bhncj', q_dec, k_inv), 0.0)
    o_intra = jnp.einsum('bhncj,bhnje->bhnce', att, v)

    def step(state, xs):
        qd, ke, vc, dec = xs
        o = jnp.einsum('bhcd,bhde->bhce', qd, state)
        state = dec[..., None] * state + jnp.einsum('bhcd,bhce->bhde', ke, vc)
        return state, o

    xs = (jnp.moveaxis(q_dec, 2, 0), jnp.moveaxis(k_end, 2, 0), jnp.moveaxis(v, 2, 0),
          jnp.moveaxis(jnp.exp(b_last[..., 0, :]), 2, 0))
    state0 = jnp.zeros((b, GLA_HEADS, GLA_HK, GLA_HV), f32)
    _, o_inter = lax.scan(step, state0, xs)
    o = o_intra + jnp.moveaxis(o_inter, 0, 2)
    o = o.transpose(0, 2, 3, 1, 4).reshape(b, s, GLA_HEADS, GLA_HV)
    gate = jax.nn.silu(r.astype(f32)).reshape(b, s, GLA_HEADS, GLA_HV)
    o = rmsnorm(o, g_norm) * gate
    return o.reshape(b, s, GLA_DV).astype(h.dtype) @ w_out


def t5_bucket(dist):
    n = jnp.maximum(dist, 0)
    nf = jnp.maximum(n, 1).astype(jnp.float32)
    large = REL_MAX_EXACT + (jnp.log(nf / REL_MAX_EXACT) / math.log(REL_MAX_DIST / REL_MAX_EXACT)
                             * (REL_BUCKETS - REL_MAX_EXACT)).astype(jnp.int32)
    large = jnp.minimum(large, REL_BUCKETS - 1)
    return jnp.where(n < REL_MAX_EXACT, n, large)


def shared_kv(x, kv_norm_g, w_kv):
    b, s, _ = x.shape
    k, v = jnp.split(rmsnorm(x, kv_norm_g) @ w_kv, 2, axis=-1)
    k = k.astype(jnp.float32).reshape(b, s, DIFF_HEADS, 2, DIFF_HD).transpose(0, 2, 3, 1, 4)
    v = v.astype(jnp.float32).reshape(b, s, DIFF_HEADS, 2 * DIFF_HD).transpose(0, 2, 1, 3)
    return k, v


def diff_mixer(h, k_sh, v_sh, rel_bias_table, w_q, lam_q1, lam_k1, lam_q2, lam_k2,
               subln_g, w_out, lambda_init):
    b, s, _ = h.shape
    f32 = jnp.float32
    q = (h @ w_q).astype(f32).reshape(b, s, DIFF_HEADS, 2, DIFF_HD) * (DIFF_HD ** -0.5)
    n_blk = s // Q_BLOCK
    q_blocks = q.reshape(b, n_blk, Q_BLOCK, DIFF_HEADS, 2, DIFF_HD).transpose(1, 0, 3, 4, 2, 5)
    lam = (jnp.exp(jnp.sum(lam_q1.astype(f32) * lam_k1.astype(f32)))
           - jnp.exp(jnp.sum(lam_q2.astype(f32) * lam_k2.astype(f32))) + lambda_init)
    table = rel_bias_table.astype(f32)
    k_pos = jnp.arange(s)

    def attend(args):
        qb, start = args
        scores = jnp.einsum('bhmqd,bhmkd->bhmqk', qb, k_sh)
        dist = (start + jnp.arange(Q_BLOCK))[:, None] - k_pos[None, :]
        bias = jnp.take(table, t5_bucket(dist), axis=0)
        scores = scores + jnp.transpose(bias, (2, 0, 1))[None, :, None]
        scores = jnp.where(dist >= 0, scores, -jnp.inf)
        p = jax.nn.softmax(scores, axis=-1)
        a = p[:, :, 0] - lam * p[:, :, 1]
        return jnp.einsum('bhqk,bhke->bhqe', a, v_sh)

    o = lax.map(attend, (q_blocks, jnp.arange(n_blk) * Q_BLOCK))
    o = o.transpose(1, 0, 3, 2, 4).reshape(b, s, DIFF_HEADS, 2 * DIFF_HD)
    o = rmsnorm(o, subln_g) * (1.0 - lambda_init)
    return o.reshape(b, s, D_MODEL).astype(h.dtype) @ w_out


def setup_inputs(seed: int = 0) -> dict:
    key = jax.random.key(seed)
    ks = jax.random.split(key, 24)
    f32 = jnp.float32

    def nrm(k, shape, scale):
        return jax.random.normal(k, shape, f32) * scale

    def gain(k, shape):
        return 1.0 + 0.05 * jax.random.normal(k, shape, f32)

    return {
        "x": nrm(ks[0], (BATCH, SEQ, D_MODEL), 1.0),
        "rel_bias_table": nrm(ks[1], (REL_BUCKETS, DIFF_HEADS), 0.5),
        "kv_norm_g": gain(ks[2], (D_MODEL,)),
        "w_kv": nrm(ks[3], (D_MODEL, 2 * D_MODEL), D_MODEL ** -0.5),
        "gla_w_in": nrm(ks[4], (N_A_LAYERS, D_MODEL, GLA_IN), D_MODEL ** -0.5),
        "gla_w_fgate": nrm(ks[5], (N_A_LAYERS, GATE_RANK, GLA_DK), GATE_RANK ** -0.5),
        "gla_b_fgate": nrm(ks[6], (N_A_LAYERS, GLA_DK), 0.1),
        "gla_norm_g": gain(ks[7], (N_A_LAYERS, GLA_HV)),
        "gla_w_out": nrm(ks[8], (N_A_LAYERS, GLA_DV, D_MODEL), GLA_DV ** -0.5),
        "diff_w_q": nrm(ks[9], (N_B_LAYERS, D_MODEL, D_MODEL), D_MODEL ** -0.5),
        "diff_lam_q1": nrm(ks[10], (N_B_LAYERS, DIFF_HD), LAMBDA_INIT_STD),
        "diff_lam_k1": nrm(ks[11], (N_B_LAYERS, DIFF_HD), LAMBDA_INIT_STD),
        "diff_lam_q2": nrm(ks[12], (N_B_LAYERS, DIFF_HD), LAMBDA_INIT_STD),
        "diff_lam_k2": nrm(ks[13], (N_B_LAYERS, DIFF_HD), LAMBDA_INIT_STD),
        "diff_subln_g": gain(ks[14], (N_B_LAYERS, 2 * DIFF_HD)),
        "diff_w_out": nrm(ks[15], (N_B_LAYERS, D_MODEL, D_MODEL), D_MODEL ** -0.5),
        "pre_mix_g": gain(ks[16], (DEPTH, D_MODEL)),
        "post_mix_g": gain(ks[17], (DEPTH, D_MODEL)),
        "pre_ffn_g": gain(ks[18], (DEPTH, D_MODEL)),
        "post_ffn_g": gain(ks[19], (DEPTH, D_MODEL)),
        "ffn_w_gate_up": nrm(ks[20], (DEPTH, D_MODEL, 2 * D_FF), D_MODEL ** -0.5),
        "ffn_w_down": nrm(ks[21], (DEPTH, D_FF, D_MODEL), D_FF ** -0.5),
    }


def reference(x, rel_bias_table, kv_norm_g, w_kv, gla_w_in, gla_w_fgate, gla_b_fgate,
              gla_norm_g, gla_w_out, diff_w_q, diff_lam_q1, diff_lam_k1, diff_lam_q2,
              diff_lam_k2, diff_subln_g, diff_w_out, pre_mix_g, post_mix_g, pre_ffn_g,
              post_ffn_g, ffn_w_gate_up, ffn_w_down):
    k_sh = None
    v_sh = None
    for i in range(DEPTH):
        h = rmsnorm(x, pre_mix_g[i])
        if i < N_A_LAYERS:
            mix = gla_mixer(h, gla_w_in[i], gla_w_fgate[i], gla_b_fgate[i],
                            gla_norm_g[i], gla_w_out[i])
        else:
            if i == N_A_LAYERS:
                k_sh, v_sh = shared_kv(x, kv_norm_g, w_kv)
            j = i - N_A_LAYERS
            lambda_init = 0.8 - 0.6 * math.exp(-0.3 * i)
            mix = diff_mixer(h, k_sh, v_sh, rel_bias_table, diff_w_q[j], diff_lam_q1[j],
                             diff_lam_k1[j], diff_lam_q2[j], diff_lam_k2[j], diff_subln_g[j],
                             diff_w_out[j], lambda_init)
        x = x + rmsnorm(mix, post_mix_g[i])
        f = swiglu(rmsnorm(x, pre_ffn_g[i]), ffn_w_gate_up[i], ffn_w_down[i])
        x = x + rmsnorm(f, post_ffn_g[i])
    return x
```

```python
import functools
import math

import jax
import jax.numpy as jnp
from jax import lax
from jax.experimental import pallas as pl
from jax.experimental.pallas import tpu as pltpu

F32 = jnp.float32
BF16 = jnp.bfloat16

D_MODEL = 2048
DEPTH = 2
N_A_LAYERS = DEPTH // 2
GLA_HEADS = 4
GLA_DK = D_MODEL // 2
GLA_DV = D_MODEL
GLA_HK = GLA_DK // GLA_HEADS
GLA_HV = GLA_DV // GLA_HEADS
GATE_RANK = 16
GATE_TAU = 16.0
GLA_CHUNK = 64
GLA_MAIN = 2 * GLA_DK + 2 * GLA_DV
DIFF_HEADS = 8
DIFF_HD = D_MODEL // DIFF_HEADS // 2
REL_BUCKETS = 32
REL_MAX_EXACT = REL_BUCKETS // 2
REL_MAX_DIST = 128
D_FF = ((8 * D_MODEL // 3 + 255) // 256) * 256
EPS = 1e-6

LANES = 128
MASK_VALUE = -1e30
VMEM_LIMIT = 56 * 1024 * 1024


def _rms(x, g):
    y = x * lax.rsqrt(jnp.mean(x * x, axis=-1, keepdims=True) + EPS)
    return y * g


def _sigmoid(x):
    return 1.0 / (1.0 + jnp.exp(-x))


def _dot(a, b):
    return jnp.dot(a, b, preferred_element_type=F32)


def _dot_nt(a, b):
    return lax.dot_general(a, b, (((1,), (1,)), ((), ())), preferred_element_type=F32)


def _dot_tn(a, b):
    return lax.dot_general(a, b, (((0,), (0,)), ((), ())), preferred_element_type=F32)


def _params(*sem):
    return pltpu.CompilerParams(dimension_semantics=sem, vmem_limit_bytes=VMEM_LIMIT)


def _norm_matmul_kernel(x_ref, g_ref, w_ref, *rest, scale, has_aux):
    if has_aux:
        wa_ref, o_ref, oa_ref, hn_ref = rest
    else:
        o_ref, hn_ref = rest
    j = pl.program_id(1)

    @pl.when(j == 0)
    def _():
        hn = _rms(x_ref[...], g_ref[...]).astype(BF16)
        hn_ref[...] = hn
        if has_aux:
            oa_ref[...] = _dot(hn, wa_ref[...]).astype(oa_ref.dtype)

    acc = _dot(hn_ref[...], w_ref[...])
    if scale != 1.0:
        acc = acc * scale
    o_ref[...] = acc.astype(o_ref.dtype)


def _norm_matmul(x, g, w, w_aux=None, *, scale=1.0, tm=1024, tn=1024, name):
    m, k = x.shape
    n = w.shape[1]
    has_aux = w_aux is not None
    in_specs = [
        pl.BlockSpec((tm, k), lambda i, j: (i, 0)),
        pl.BlockSpec((1, k), lambda i, j: (0, 0)),
        pl.BlockSpec((k, tn), lambda i, j: (0, j)),
    ]
    out_shape = [jax.ShapeDtypeStruct((m, n), BF16)]
    out_specs = [pl.BlockSpec((tm, tn), lambda i, j: (i, j))]
    args = [x, g.reshape(1, k), w]
    if has_aux:
        na = w_aux.shape[1]
        in_specs.append(pl.BlockSpec((k, na), lambda i, j: (0, 0)))
        out_shape.append(jax.ShapeDtypeStruct((m, na), BF16))
        out_specs.append(pl.BlockSpec((tm, na), lambda i, j: (i, 0)))
        args.append(w_aux)
    res = pl.pallas_call(
        functools.partial(_norm_matmul_kernel, scale=scale, has_aux=has_aux),
        grid=(m // tm, n // tn),
        in_specs=in_specs,
        out_specs=out_specs,
        out_shape=out_shape,
        scratch_shapes=[pltpu.VMEM((tm, k), BF16)],
        compiler_params=_params("parallel", "arbitrary"),
        name=name,
    )(*args)
    return res if has_aux else res[0]


def _gla_kernel(q_ref, k_ref, v_ref, r_ref, glr_ref, wf_ref, bf_ref, gn_ref, o_ref, st_ref):
    c_len = GLA_CHUNK
    n_chunks = q_ref.shape[1] // c_len
    row = lax.broadcasted_iota(jnp.int32, (c_len, c_len), 0)
    col = lax.broadcasted_iota(jnp.int32, (c_len, c_len), 1)
    causal = row >= col
    tril = jnp.where(causal, 1.0, 0.0).astype(BF16)
    wf = wf_ref[...]
    bf = bf_ref[...]
    gn = gn_ref[...]
    st_ref[...] = jnp.zeros_like(st_ref)

    def chunk(c, carry):
        r0 = pl.multiple_of(c * c_len, c_len)
        rows = pl.ds(r0, c_len)
        q = q_ref[0, rows, :].astype(F32) * (GLA_HK ** -0.5)
        k = k_ref[0, rows, :].astype(F32)
        v = v_ref[0, rows, :]
        z = _dot(glr_ref[0, rows, :], wf) + bf
        log_a = (jnp.minimum(z, 0.0) - jnp.log1p(jnp.exp(-jnp.abs(z)))) / GATE_TAU
        la_hi = log_a.astype(BF16)
        la_lo = (log_a - la_hi.astype(F32)).astype(BF16)
        bcum = _dot(tril, la_hi) + _dot(tril, la_lo)
        b_last = bcum[c_len - 1:c_len, :]
        q_dec = (q * jnp.exp(bcum)).astype(BF16)
        k_inv = (k * jnp.exp(-bcum)).astype(BF16)
        k_end = (k * jnp.exp(b_last - bcum)).astype(BF16)
        att = jnp.where(causal, _dot_nt(q_dec, k_inv), 0.0).astype(BF16)
        st = st_ref[...]
        o = _dot(att, v) + _dot_nt(q_dec, st.astype(BF16))
        st_ref[...] = st * jnp.exp(b_last) + _dot_tn(v, k_end)
        gate = r_ref[0, rows, :].astype(F32)
        gate = gate * _sigmoid(gate)
        o_ref[0, rows, :] = (_rms(o, gn) * gate).astype(o_ref.dtype)
        return carry

    lax.fori_loop(0, n_chunks, chunk, 0)


def _gla(proj, glr, wf, bfg, gn):
    b, s, _ = proj.shape
    hk, hv = GLA_HK, GLA_HV
    k_off = GLA_DK // hk
    v_off = 2 * GLA_DK // hv
    r_off = (2 * GLA_DK + GLA_DV) // hv
    return pl.pallas_call(
        _gla_kernel,
        grid=(b, GLA_HEADS),
        in_specs=[
            pl.BlockSpec((1, s, hk), lambda i, h: (i, 0, h)),
            pl.BlockSpec((1, s, hk), lambda i, h: (i, 0, k_off + h)),
            pl.BlockSpec((1, s, hv), lambda i, h: (i, 0, v_off + h)),
            pl.BlockSpec((1, s, hv), lambda i, h: (i, 0, r_off + h)),
            pl.BlockSpec((1, s, LANES), lambda i, h: (i, 0, 0)),
            pl.BlockSpec((LANES, hk), lambda i, h: (0, h)),
            pl.BlockSpec((1, hk), lambda i, h: (0, h)),
            pl.BlockSpec((1, hv), lambda i, h: (0, 0)),
        ],
        out_specs=pl.BlockSpec((1, s, hv), lambda i, h: (i, 0, h)),
        out_shape=jax.ShapeDtypeStruct((b, s, GLA_DV), BF16),
        scratch_shapes=[pltpu.VMEM((hv, hk), F32)],
        compiler_params=_params("parallel", "parallel"),
        name="gla_mixer",
    )(proj, proj, proj, proj, glr, wf, bfg, gn)


def _out_proj_kernel(a_ref, w_ref, g_ref, x_ref, o_ref):
    mix = _dot(a_ref[...], w_ref[...])
    o_ref[...] = x_ref[...] + _rms(mix, g_ref[...])


def _out_proj(a, w, g, x, *, tm=512, name):
    m, k = a.shape
    n = w.shape[1]
    return pl.pallas_call(
        _out_proj_kernel,
        grid=(m // tm,),
        in_specs=[
            pl.BlockSpec((tm, k), lambda i: (i, 0)),
            pl.BlockSpec((k, n), lambda i: (0, 0)),
            pl.BlockSpec((1, n), lambda i: (0, 0)),
            pl.BlockSpec((tm, n), lambda i: (i, 0)),
        ],
        out_specs=pl.BlockSpec((tm, n), lambda i: (i, 0)),
        out_shape=jax.ShapeDtypeStruct((m, n), F32),
        compiler_params=_params("parallel"),
        name=name,
    )(a, w, g.reshape(1, n), x)


def _ffn_kernel(x_ref, gpre_ref, wg_ref, wu_ref, wd_ref, gpost_ref, o_ref, hn_ref):
    f = pl.program_id(1)

    @pl.when(f == 0)
    def _():
        hn_ref[...] = _rms(x_ref[...], gpre_ref[...]).astype(BF16)
        o_ref[...] = jnp.zeros_like(o_ref)

    h = hn_ref[...]
    gate = _dot(h, wg_ref[...])
    up = _dot(h, wu_ref[...])
    act = (gate * _sigmoid(gate) * up).astype(BF16)
    o_ref[...] += _dot(act, wd_ref[...])

    @pl.when(f == pl.num_programs(1) - 1)
    def _():
        o_ref[...] = x_ref[...] + _rms(o_ref[...], gpost_ref[...])


def _ffn(x, g_pre, w_gate_up, w_down, g_post, *, tm=1024, tf=256, name):
    m, d = x.shape
    dff = w_down.shape[0]
    nf = dff // tf
    return pl.pallas_call(
        _ffn_kernel,
        grid=(m // tm, nf),
        in_specs=[
            pl.BlockSpec((tm, d), lambda i, f: (i, 0)),
            pl.BlockSpec((1, d), lambda i, f: (0, 0)),
            pl.BlockSpec((d, tf), lambda i, f: (0, f)),
            pl.BlockSpec((d, tf), lambda i, f: (0, nf + f)),
            pl.BlockSpec((tf, d), lambda i, f: (f, 0)),
            pl.BlockSpec((1, d), lambda i, f: (0, 0)),
        ],
        out_specs=pl.BlockSpec((tm, d), lambda i, f: (i, 0)),
        out_shape=jax.ShapeDtypeStruct((m, d), F32),
        scratch_shapes=[pltpu.VMEM((tm, d), BF16)],
        compiler_params=_params("parallel", "arbitrary"),
        name=name,
    )(x, g_pre.reshape(1, d), w_gate_up, w_gate_up, w_down, g_post.reshape(1, d))


def _t5_bias_tile(dist, table_ref, h):
    n = jnp.maximum(dist, 0)
    nf = jnp.maximum(n, 1).astype(F32)
    large = REL_MAX_EXACT + (jnp.log(nf / REL_MAX_EXACT) / math.log(REL_MAX_DIST / REL_MAX_EXACT)
                             * (REL_BUCKETS - REL_MAX_EXACT)).astype(jnp.int32)
    large = jnp.minimum(large, REL_BUCKETS - 1)
    bucket = jnp.where(n < REL_MAX_EXACT, n, large)
    bias = jnp.zeros(dist.shape, F32)
    for b in range(REL_BUCKETS):
        bias = jnp.where(bucket == b, table_ref[b, h], bias)
    return jnp.where(dist >= 0, bias, MASK_VALUE)


def _attn_kernel(table_ref, lq1_ref, lk1_ref, lq2_ref, lk2_ref, q_ref, k_ref, v_ref, g_ref,
                 o_ref, bias_ref, m_ref, l_ref, acc_ref, *, lambda_init, tq, tk):
    h = pl.program_id(0)
    b = pl.program_id(1)
    qi = pl.program_id(2)
    hd = DIFF_HD

    @pl.when((b == 0) & (qi == 0))
    def _():
        rel = (lax.broadcasted_iota(jnp.int32, (tq, tk), 0)
               - lax.broadcasted_iota(jnp.int32, (tq, tk), 1))
        for r in range(3):
            bias_ref[r] = _t5_bias_tile(rel + r * tk, table_ref, h)

    lam = (jnp.exp(jnp.sum(lq1_ref[...] * lk1_ref[...], axis=-1, keepdims=True))
           - jnp.exp(jnp.sum(lq2_ref[...] * lk2_ref[...], axis=-1, keepdims=True))
           + lambda_init)

    m_ref[...] = jnp.full(m_ref.shape, MASK_VALUE, F32)
    l_ref[...] = jnp.zeros_like(l_ref)
    acc_ref[...] = jnp.zeros_like(acc_ref)
    q = q_ref[0]

    def kv_step(kb, carry):
        rows = pl.ds(pl.multiple_of(kb * tk, tk), tk)
        kblk = k_ref[0, rows, :]
        vblk = v_ref[0, rows, :]
        bias = bias_ref[jnp.minimum(qi - kb, 2)]
        for br in range(2):
            s = _dot_nt(q[:, br * hd:(br + 1) * hd], kblk[:, br * hd:(br + 1) * hd]) + bias
            m_prev = m_ref[br]
            m_new = jnp.maximum(m_prev, jnp.max(s, axis=-1, keepdims=True))
            alpha = jnp.exp(m_prev - m_new)
            p = jnp.exp(s - m_new)
            l_ref[br] = alpha * l_ref[br] + jnp.sum(p, axis=-1, keepdims=True)
            acc_ref[br] = alpha * acc_ref[br] + _dot(p.astype(BF16), vblk)
            m_ref[br] = m_new
        return carry

    lax.fori_loop(0, qi + 1, kv_step, 0)

    o = acc_ref[0] / l_ref[0] - lam * (acc_ref[1] / l_ref[1])
    o_ref[0] = (_rms(o, g_ref[...]) * (1.0 - lambda_init)).astype(o_ref.dtype)


def _attention(table, lq1, lk1, lq2, lk2, q, kv, g, *, lambda_init, tq=256, tk=256):
    b, s, _ = q.shape
    hw = 2 * DIFF_HD
    v_off = D_MODEL // hw
    vec = pl.BlockSpec((1, DIFF_HD), lambda h, i, j: (0, 0))
    return pl.pallas_call(
        functools.partial(_attn_kernel, lambda_init=lambda_init, tq=tq, tk=tk),
        grid=(DIFF_HEADS, b, s // tq),
        in_specs=[
            pl.BlockSpec(memory_space=pltpu.SMEM),
            vec, vec, vec, vec,
            pl.BlockSpec((1, tq, hw), lambda h, i, j: (i, j, h)),
            pl.BlockSpec((1, s, hw), lambda h, i, j: (i, 0, h)),
            pl.BlockSpec((1, s, hw), lambda h, i, j: (i, 0, v_off + h)),
            pl.BlockSpec((1, hw), lambda h, i, j: (0, 0)),
        ],
        out_specs=pl.BlockSpec((1, tq, hw), lambda h, i, j: (i, j, h)),
        out_shape=jax.ShapeDtypeStruct((b, s, D_MODEL), BF16),
        scratch_shapes=[
            pltpu.VMEM((3, tq, tk), F32),
            pltpu.VMEM((2, tq, 1), F32),
            pltpu.VMEM((2, tq, 1), F32),
            pltpu.VMEM((2, tq, hw), F32),
        ],
        compiler_params=_params("arbitrary", "arbitrary", "arbitrary"),
        name="diff_attention",
    )(table, lq1.reshape(1, -1), lk1.reshape(1, -1), lq2.reshape(1, -1), lk2.reshape(1, -1),
      q, kv, kv, g.reshape(1, hw))


def kernel(x, rel_bias_table, kv_norm_g, w_kv, gla_w_in, gla_w_fgate, gla_b_fgate, gla_norm_g, gla_w_out, diff_w_q, diff_lam_q1, diff_lam_k1, diff_lam_q2, diff_lam_k2, diff_subln_g, diff_w_out, pre_mix_g, post_mix_g, pre_ffn_g, post_ffn_g, ffn_w_gate_up, ffn_w_down):
    b, s, d = x.shape
    m = b * s
    xf = x.reshape(m, d)

    w_in = gla_w_in[0]
    w_main = w_in[:, :GLA_MAIN].astype(BF16)
    w_glr = jnp.pad(w_in[:, GLA_MAIN:], ((0, 0), (0, LANES - GATE_RANK))).astype(BF16)
    proj, glr = _norm_matmul(xf, pre_mix_g[0], w_main, w_glr, name="gla_in_proj")
    wf = jnp.pad(gla_w_fgate[0], ((0, LANES - GATE_RANK), (0, 0))).astype(BF16)
    o = _gla(proj.reshape(b, s, GLA_MAIN), glr.reshape(b, s, LANES), wf,
             gla_b_fgate[0].reshape(1, GLA_DK), gla_norm_g[0].reshape(1, GLA_HV))
    xf = _out_proj(o.reshape(m, d), gla_w_out[0].astype(BF16), post_mix_g[0], xf,
                   name="gla_out_proj")
    xf = _ffn(xf, pre_ffn_g[0], ffn_w_gate_up[0].astype(BF16), ffn_w_down[0].astype(BF16),
              post_ffn_g[0], name="ffn0")

    i = N_A_LAYERS
    lambda_init = 0.8 - 0.6 * math.exp(-0.3 * i)
    kv = _norm_matmul(xf, kv_norm_g, w_kv.astype(BF16), name="shared_kv_proj")
    q = _norm_matmul(xf, pre_mix_g[i], diff_w_q[0].astype(BF16), scale=DIFF_HD ** -0.5,
                     name="diff_q_proj")
    o = _attention(rel_bias_table, diff_lam_q1[0], diff_lam_k1[0], diff_lam_q2[0], diff_lam_k2[0],
                   q.reshape(b, s, d), kv.reshape(b, s, 2 * d), diff_subln_g[0],
                   lambda_init=lambda_init)
    xf = _out_proj(o.reshape(m, d), diff_w_out[0].astype(BF16), post_mix_g[i], xf,
                   name="diff_out_proj")
    xf = _ffn(xf, pre_ffn_g[i], ffn_w_gate_up[i].astype(BF16), ffn_w_down[i].astype(BF16),
              post_ffn_g[i], name="ffn1")
    return xf.reshape(b, s, d)
```

```python
import functools
import math

import jax
import jax.numpy as jnp
from jax import lax
from jax.experimental import pallas as pl
from jax.experimental.pallas import tpu as pltpu

F32 = jnp.float32
BF16 = jnp.bfloat16

D_MODEL = 2048
DEPTH = 2
N_A_LAYERS = DEPTH // 2
GLA_HEADS = 4
GLA_DK = D_MODEL // 2
GLA_DV = D_MODEL
GLA_HK = GLA_DK // GLA_HEADS
GLA_HV = GLA_DV // GLA_HEADS
GATE_RANK = 16
GATE_TAU = 16.0
GLA_CHUNK = 64
GLA_MAIN = 2 * GLA_DK + 2 * GLA_DV
DIFF_HEADS = 8
DIFF_HD = D_MODEL // DIFF_HEADS // 2
REL_BUCKETS = 32
REL_MAX_EXACT = REL_BUCKETS // 2
REL_MAX_DIST = 128
D_FF = ((8 * D_MODEL // 3 + 255) // 256) * 256
EPS = 1e-6

LANES = 128
MASK_VALUE = -1e30
VMEM_LIMIT = 56 * 1024 * 1024


def _rms(x, g):
    y = x * lax.rsqrt(jnp.mean(x * x, axis=-1, keepdims=True) + EPS)
    return y * g


def _sigmoid(x):
    return 1.0 / (1.0 + jnp.exp(-x))


def _dot(a, b):
    return jnp.dot(a, b, preferred_element_type=F32)


def _dot_nt(a, b):
    return lax.dot_general(a, b, (((1,), (1,)), ((), ())), preferred_element_type=F32)


def _dot_tn(a, b):
    return lax.dot_general(a, b, (((0,), (0,)), ((), ())), preferred_element_type=F32)


def _params(*sem):
    return pltpu.CompilerParams(dimension_semantics=sem, vmem_limit_bytes=VMEM_LIMIT)


def _norm_matmul_kernel(x_ref, g_ref, w_ref, *rest, scale, has_aux):
    if has_aux:
        wa_ref, o_ref, oa_ref, hn_ref = rest
    else:
        o_ref, hn_ref = rest
    j = pl.program_id(1)

    @pl.when(j == 0)
    def _():
        hn = _rms(x_ref[...], g_ref[...]).astype(BF16)
        hn_ref[...] = hn
        if has_aux:
            oa_ref[...] = _dot(hn, wa_ref[...]).astype(oa_ref.dtype)

    acc = _dot(hn_ref[...], w_ref[...])
    if scale != 1.0:
        acc = acc * scale
    o_ref[...] = acc.astype(o_ref.dtype)


def _norm_matmul(x, g, w, w_aux=None, *, scale=1.0, tm=1024, tn=1024, name):
    m, k = x.shape
    n = w.shape[1]
    has_aux = w_aux is not None
    in_specs = [
        pl.BlockSpec((tm, k), lambda i, j: (i, 0)),
        pl.BlockSpec((1, k), lambda i, j: (0, 0)),
        pl.BlockSpec((k, tn), lambda i, j: (0, j)),
    ]
    out_shape = [jax.ShapeDtypeStruct((m, n), BF16)]
    out_specs = [pl.BlockSpec((tm, tn), lambda i, j: (i, j))]
    args = [x, g.reshape(1, k), w]
    if has_aux:
        na = w_aux.shape[1]
        in_specs.append(pl.BlockSpec((k, na), lambda i, j: (0, 0)))
        out_shape.append(jax.ShapeDtypeStruct((m, na), BF16))
        out_specs.append(pl.BlockSpec((tm, na), lambda i, j: (i, 0)))
        args.append(w_aux)
    res = pl.pallas_call(
        functools.partial(_norm_matmul_kernel, scale=scale, has_aux=has_aux),
        grid=(m // tm, n // tn),
        in_specs=in_specs,
        out_specs=out_specs,
        out_shape=out_shape,
        scratch_shapes=[pltpu.VMEM((tm, k), BF16)],
        compiler_params=_params("parallel", "arbitrary"),
        name=name,
    )(*args)
    return res if has_aux else res[0]


def _gla_kernel(q_ref, k_ref, v_ref, r_ref, glr_ref, wf_ref, bf_ref, gn_ref, o_ref, st_ref):
    c_len = GLA_CHUNK
    n_chunks = q_ref.shape[1] // c_len
    row = lax.broadcasted_iota(jnp.int32, (c_len, c_len), 0)
    col = lax.broadcasted_iota(jnp.int32, (c_len, c_len), 1)
    causal = row >= col
    tril = jnp.where(causal, 1.0, 0.0).astype(BF16)
    wf = wf_ref[...]
    bf = bf_ref[...]
    gn = gn_ref[...]
    st_ref[...] = jnp.zeros_like(st_ref)

    def chunk(c, carry):
        r0 = pl.multiple_of(c * c_len, c_len)
        rows = pl.ds(r0, c_len)
        q = q_ref[0, rows, :].astype(F32) * (GLA_HK ** -0.5)
        k = k_ref[0, rows, :].astype(F32)
        v = v_ref[0, rows, :]
        z = _dot(glr_ref[0, rows, :], wf) + bf
        log_a = (jnp.minimum(z, 0.0) - jnp.log1p(jnp.exp(-jnp.abs(z)))) / GATE_TAU
        la_hi = log_a.astype(BF16)
        la_lo = (log_a - la_hi.astype(F32)).astype(BF16)
        bcum = _dot(tril, la_hi) + _dot(tril, la_lo)
        b_last = bcum[c_len - 1:c_len, :]
        q_dec = (q * jnp.exp(bcum)).astype(BF16)
        k_inv = (k * jnp.exp(-bcum)).astype(BF16)
        k_end = (k * jnp.exp(b_last - bcum)).astype(BF16)
        att = jnp.where(causal, _dot_nt(q_dec, k_inv), 0.0).astype(BF16)
        st = st_ref[...]
        o = _dot(att, v) + _dot_nt(q_dec, st.astype(BF16))
        st_ref[...] = st * jnp.exp(b_last) + _dot_tn(v, k_end)
        gate = r_ref[0, rows, :].astype(F32)
        gate = gate * _sigmoid(gate)
        o_ref[0, rows, :] = (_rms(o, gn) * gate).astype(o_ref.dtype)
        return carry

    lax.fori_loop(0, n_chunks, chunk, 0)


def _gla(proj, glr, wf, bfg, gn):
    b, s, _ = proj.shape
    hk, hv = GLA_HK, GLA_HV
    k_off = GLA_DK // hk
    v_off = 2 * GLA_DK // hv
    r_off = (2 * GLA_DK + GLA_DV) // hv
    return pl.pallas_call(
        _gla_kernel,
        grid=(b, GLA_HEADS),
        in_specs=[
            pl.BlockSpec((1, s, hk), lambda i, h: (i, 0, h)),
            pl.BlockSpec((1, s, hk), lambda i, h: (i, 0, k_off + h)),
            pl.BlockSpec((1, s, hv), lambda i, h: (i, 0, v_off + h)),
            pl.BlockSpec((1, s, hv), lambda i, h: (i, 0, r_off + h)),
            pl.BlockSpec((1, s, LANES), lambda i, h: (i, 0, 0)),
            pl.BlockSpec((LANES, hk), lambda i, h: (0, h)),
            pl.BlockSpec((1, hk), lambda i, h: (0, h)),
            pl.BlockSpec((1, hv), lambda i, h: (0, 0)),
        ],
        out_specs=pl.BlockSpec((1, s, hv), lambda i, h: (i, 0, h)),
        out_shape=jax.ShapeDtypeStruct((b, s, GLA_DV), BF16),
        scratch_shapes=[pltpu.VMEM((hv, hk), F32)],
        compiler_params=_params("parallel", "parallel"),
        name="gla_mixer",
    )(proj, proj, proj, proj, glr, wf, bfg, gn)


def _out_proj_kernel(a_ref, w_ref, g_ref, x_ref, o_ref):
    mix = _dot(a_ref[...], w_ref[...])
    o_ref[...] = x_ref[...] + _rms(mix, g_ref[...])


def _out_proj(a, w, g, x, *, tm=512, name):
    m, k = a.shape
    n = w.shape[1]
    return pl.pallas_call(
        _out_proj_kernel,
        grid=(m // tm,),
        in_specs=[
            pl.BlockSpec((tm, k), lambda i: (i, 0)),
            pl.BlockSpec((k, n), lambda i: (0, 0)),
            pl.BlockSpec((1, n), lambda i: (0, 0)),
            pl.BlockSpec((tm, n), lambda i: (i, 0)),
        ],
        out_specs=pl.BlockSpec((tm, n), lambda i: (i, 0)),
        out_shape=jax.ShapeDtypeStruct((m, n), F32),
        compiler_params=_params("parallel"),
        name=name,
    )(a, w, g.reshape(1, n), x)


def _ffn_kernel(x_ref, gpre_ref, wg_ref, wu_ref, wd_ref, gpost_ref, o_ref, hn_ref):
    f = pl.program_id(1)

    @pl.when(f == 0)
    def _():
        hn_ref[...] = _rms(x_ref[...], gpre_ref[...]).astype(BF16)
        o_ref[...] = jnp.zeros_like(o_ref)

    h = hn_ref[...]
    gate = _dot(h, wg_ref[...])
    up = _dot(h, wu_ref[...])
    act = (gate * _sigmoid(gate) * up).astype(BF16)
    o_ref[...] += _dot(act, wd_ref[...])

    @pl.when(f == pl.num_programs(1) - 1)
    def _():
        o_ref[...] = x_ref[...] + _rms(o_ref[...], gpost_ref[...])


def _ffn(x, g_pre, w_gate_up, w_down, g_post, *, tm=1024, tf=256, name):
    m, d = x.shape
    dff = w_down.shape[0]
    nf = dff // tf
    return pl.pallas_call(
        _ffn_kernel,
        grid=(m // tm, nf),
        in_specs=[
            pl.BlockSpec((tm, d), lambda i, f: (i, 0)),
            pl.BlockSpec((1, d), lambda i, f: (0, 0)),
            pl.BlockSpec((d, tf), lambda i, f: (0, f)),
            pl.BlockSpec((d, tf), lambda i, f: (0, nf + f)),
            pl.BlockSpec((tf, d), lambda i, f: (f, 0)),
            pl.BlockSpec((1, d), lambda i, f: (0, 0)),
        ],
        out_specs=pl.BlockSpec((tm, d), lambda i, f: (i, 0)),
        out_shape=jax.ShapeDtypeStruct((m, d), F32),
        scratch_shapes=[pltpu.VMEM((tm, d), BF16)],
        compiler_params=_params("parallel", "arbitrary"),
        name=name,
    )(x, g_pre.reshape(1, d), w_gate_up, w_gate_up, w_down, g_post.reshape(1, d))


def _t5_bias_tile(dist, table_ref, h):
    n = jnp.maximum(dist, 0)
    nf = jnp.maximum(n, 1).astype(F32)
    large = REL_MAX_EXACT + (jnp.log(nf / REL_MAX_EXACT) / math.log(REL_MAX_DIST / REL_MAX_EXACT)
                             * (REL_BUCKETS - REL_MAX_EXACT)).astype(jnp.int32)
    large = jnp.minimum(large, REL_BUCKETS - 1)
    bucket = jnp.where(n < REL_MAX_EXACT, n, large)
    far = table_ref[REL_BUCKETS - 1, h]
    bias = jnp.zeros(dist.shape, F32)
    for b in range(REL_BUCKETS - 1):
        bias = jnp.where(bucket == b, table_ref[b, h] - far, bias)
    return jnp.where(dist >= 0, bias, MASK_VALUE)


def _attn_kernel(table_ref, lq1_ref, lk1_ref, lq2_ref, lk2_ref, q_ref, k_ref, v_ref, g_ref,
                 o_ref, bias_ref, *, lambda_init, tq):
    h = pl.program_id(0)
    b = pl.program_id(1)
    hd = DIFF_HD
    n_q = q_ref.shape[1] // tq

    @pl.when(b == 0)
    def _():
        rel = (lax.broadcasted_iota(jnp.int32, (tq, tq), 0)
               - lax.broadcasted_iota(jnp.int32, (tq, tq), 1))
        for r in range(2):
            bias_ref[r] = _t5_bias_tile(rel + r * tq, table_ref, h)

    lam = (jnp.exp(jnp.sum(lq1_ref[...] * lk1_ref[...], axis=-1, keepdims=True))
           - jnp.exp(jnp.sum(lq2_ref[...] * lk2_ref[...], axis=-1, keepdims=True))
           + lambda_init)
    g = g_ref[...]

    for qi in range(n_q):
        q_rows = slice(qi * tq, (qi + 1) * tq)
        segs = []
        if qi >= 2:
            segs.append((slice(0, (qi - 1) * tq), None))
        if qi >= 1:
            segs.append((slice((qi - 1) * tq, qi * tq), 1))
        segs.append((q_rows, 0))

        probs = []
        for br in range(2):
            cols = slice(br * hd, (br + 1) * hd)
            qb = q_ref[0, q_rows, cols]
            s = []
            for k_rows, r in segs:
                sc = _dot_nt(qb, k_ref[0, k_rows, cols])
                s.append(sc if r is None else sc + bias_ref[r])
            m = functools.reduce(jnp.maximum, [jnp.max(sc, axis=-1, keepdims=True) for sc in s])
            p = [jnp.exp(sc - m) for sc in s]
            l = functools.reduce(jnp.add, [jnp.sum(pc, axis=-1, keepdims=True) for pc in p])
            probs.append((p, l))

        c1 = 1.0 / probs[0][1]
        c2 = lam / probs[1][1]
        o = None
        for idx, (k_rows, _) in enumerate(segs):
            a = (probs[0][0][idx] * c1 - probs[1][0][idx] * c2).astype(BF16)
            part = _dot(a, v_ref[0, k_rows, :])
            o = part if o is None else o + part
        o_ref[0, q_rows, :] = (_rms(o, g) * (1.0 - lambda_init)).astype(o_ref.dtype)


def _attention(table, lq1, lk1, lq2, lk2, q, kv, g, *, lambda_init, tq=256):
    b, s, _ = q.shape
    hw = 2 * DIFF_HD
    v_off = D_MODEL // hw
    vec = pl.BlockSpec((1, DIFF_HD), lambda h, i: (0, 0))
    return pl.pallas_call(
        functools.partial(_attn_kernel, lambda_init=lambda_init, tq=tq),
        grid=(DIFF_HEADS, b),
        in_specs=[
            pl.BlockSpec(memory_space=pltpu.SMEM),
            vec, vec, vec, vec,
            pl.BlockSpec((1, s, hw), lambda h, i: (i, 0, h)),
            pl.BlockSpec((1, s, hw), lambda h, i: (i, 0, h)),
            pl.BlockSpec((1, s, hw), lambda h, i: (i, 0, v_off + h)),
            pl.BlockSpec((1, hw), lambda h, i: (0, 0)),
        ],
        out_specs=pl.BlockSpec((1, s, hw), lambda h, i: (i, 0, h)),
        out_shape=jax.ShapeDtypeStruct((b, s, D_MODEL), BF16),
        scratch_shapes=[pltpu.VMEM((2, tq, tq), F32)],
        compiler_params=_params("arbitrary", "arbitrary"),
        name="diff_attention",
    )(table, lq1.reshape(1, -1), lk1.reshape(1, -1), lq2.reshape(1, -1), lk2.reshape(1, -1),
      q, kv, kv, g.reshape(1, hw))


def kernel(x, rel_bias_table, kv_norm_g, w_kv, gla_w_in, gla_w_fgate, gla_b_fgate, gla_norm_g, gla_w_out, diff_w_q, diff_lam_q1, diff_lam_k1, diff_lam_q2, diff_lam_k2, diff_subln_g, diff_w_out, pre_mix_g, post_mix_g, pre_ffn_g, post_ffn_g, ffn_w_gate_up, ffn_w_down):
    b, s, d = x.shape
    m = b * s
    xf = x.reshape(m, d)

    w_in = gla_w_in[0]
    w_main = w_in[:, :GLA_MAIN].astype(BF16)
    w_glr = jnp.pad(w_in[:, GLA_MAIN:], ((0, 0), (0, LANES - GATE_RANK))).astype(BF16)
    proj, glr = _norm_matmul(xf, pre_mix_g[0], w_main, w_glr, name="gla_in_proj")
    wf = jnp.pad(gla_w_fgate[0], ((0, LANES - GATE_RANK), (0, 0))).astype(BF16)
    o = _gla(proj.reshape(b, s, GLA_MAIN), glr.reshape(b, s, LANES), wf,
             gla_b_fgate[0].reshape(1, GLA_DK), gla_norm_g[0].reshape(1, GLA_HV))
    xf = _out_proj(o.reshape(m, d), gla_w_out[0].astype(BF16), post_mix_g[0], xf,
                   name="gla_out_proj")
    xf = _ffn(xf, pre_ffn_g[0], ffn_w_gate_up[0].astype(BF16), ffn_w_down[0].astype(BF16),
              post_ffn_g[0], name="ffn0")

    i = N_A_LAYERS
    lambda_init = 0.8 - 0.6 * math.exp(-0.3 * i)
    kv = _norm_matmul(xf, kv_norm_g, w_kv.astype(BF16), name="shared_kv_proj")
    q = _norm_matmul(xf, pre_mix_g[i], diff_w_q[0].astype(BF16), scale=DIFF_HD ** -0.5,
                     name="diff_q_proj")
    o = _attention(rel_bias_table, diff_lam_q1[0], diff_lam_k1[0], diff_lam_q2[0], diff_lam_k2[0],
                   q.reshape(b, s, d), kv.reshape(b, s, 2 * d), diff_subln_g[0],
                   lambda_init=lambda_init)
    xf = _out_proj(o.reshape(m, d), diff_w_out[0].astype(BF16), post_mix_g[i], xf,
                   name="diff_out_proj")
    xf = _ffn(xf, pre_ffn_g[i], ffn_w_gate_up[i].astype(BF16), ffn_w_down[i].astype(BF16),
              post_ffn_g[i], name="ffn1")
    return xf.reshape(b, s, d)
```

```python
import functools
import math

import jax
import jax.numpy as jnp
from jax import lax
from jax.experimental import pallas as pl
from jax.experimental.pallas import tpu as pltpu

F32 = jnp.float32
BF16 = jnp.bfloat16

D_MODEL = 2048
DEPTH = 2
N_A_LAYERS = DEPTH // 2
GLA_HEADS = 4
GLA_DK = D_MODEL // 2
GLA_DV = D_MODEL
GLA_HK = GLA_DK // GLA_HEADS
GLA_HV = GLA_DV // GLA_HEADS
GATE_RANK = 16
GATE_TAU = 16.0
GLA_CHUNK = 64
GLA_MAIN = 2 * GLA_DK + 2 * GLA_DV
DIFF_HEADS = 8
DIFF_HD = D_MODEL // DIFF_HEADS // 2
REL_BUCKETS = 32
REL_MAX_EXACT = REL_BUCKETS // 2
REL_MAX_DIST = 128
D_FF = ((8 * D_MODEL // 3 + 255) // 256) * 256
EPS = 1e-6

LANES = 128
MASK_VALUE = -1e30
VMEM_LIMIT = 56 * 1024 * 1024


def _rms(x, g):
    y = x * lax.rsqrt(jnp.mean(x * x, axis=-1, keepdims=True) + EPS)
    return y * g


def _sigmoid(x):
    return 1.0 / (1.0 + jnp.exp(-x))


def _dot(a, b):
    return jnp.dot(a, b, preferred_element_type=F32)


def _dot_nt(a, b):
    return lax.dot_general(a, b, (((1,), (1,)), ((), ())), preferred_element_type=F32)


def _dot_tn(a, b):
    return lax.dot_general(a, b, (((0,), (0,)), ((), ())), preferred_element_type=F32)


def _params(*sem):
    return pltpu.CompilerParams(dimension_semantics=sem, vmem_limit_bytes=VMEM_LIMIT)


def _norm_matmul_kernel(x_ref, g_ref, w_ref, *rest, scale, split, has_aux):
    if has_aux:
        wa_ref, o_ref, oa_ref, hn_ref = rest
    else:
        o_ref, hn_ref = rest
    j = pl.program_id(1)

    @pl.when((j == 0) | (j == split))
    def _():
        hn_ref[...] = _rms(x_ref[...], g_ref[...]).astype(BF16)

    if has_aux:
        @pl.when(j == 0)
        def _():
            oa_ref[...] = _dot(hn_ref[...], wa_ref[...]).astype(oa_ref.dtype)

    acc = _dot(hn_ref[...], w_ref[...])
    if scale != 1.0:
        acc = acc * jnp.where(j < split, scale, 1.0)
    o_ref[...] = acc.astype(o_ref.dtype)


def _norm_matmul(x, g, w, w_aux=None, *, scale=1.0, split=None, tm=1024, tn=1024, name):
    m, k = x.shape
    n = w.shape[1]
    has_aux = w_aux is not None
    n_tiles = n // tn
    split = n_tiles if split is None else split // tn
    in_specs = [
        pl.BlockSpec((tm, k), lambda i, j: (i, 0)),
        pl.BlockSpec((None, 1, k), lambda i, j: (jnp.where(j < split, 0, 1), 0, 0)),
        pl.BlockSpec((k, tn), lambda i, j: (0, j)),
    ]
    out_shape = [jax.ShapeDtypeStruct((m, n), BF16)]
    out_specs = [pl.BlockSpec((tm, tn), lambda i, j: (i, j))]
    args = [x, g.reshape(-1, 1, k), w]
    if has_aux:
        na = w_aux.shape[1]
        in_specs.append(pl.BlockSpec((k, na), lambda i, j: (0, 0)))
        out_shape.append(jax.ShapeDtypeStruct((m, na), BF16))
        out_specs.append(pl.BlockSpec((tm, na), lambda i, j: (i, 0)))
        args.append(w_aux)
    res = pl.pallas_call(
        functools.partial(_norm_matmul_kernel, scale=scale, split=split, has_aux=has_aux),
        grid=(m // tm, n // tn),
        in_specs=in_specs,
        out_specs=out_specs,
        out_shape=out_shape,
        scratch_shapes=[pltpu.VMEM((tm, k), BF16)],
        compiler_params=_params("parallel", "arbitrary"),
        name=name,
    )(*args)
    return res if has_aux else res[0]


def _gla_kernel(q_ref, k_ref, v_ref, r_ref, glr_ref, wf_ref, bf_ref, gn_ref, o_ref, st_ref, *,
                group):
    c_len = GLA_CHUNK
    t = group * c_len
    n_tiles = q_ref.shape[1] // t
    row = lax.broadcasted_iota(jnp.int32, (t, t), 0)
    col = lax.broadcasted_iota(jnp.int32, (t, t), 1)
    same_chunk = (row // c_len) == (col // c_len)
    causal = same_chunk & (row >= col)
    sum_mat = jnp.concatenate([jnp.where(causal, 1.0, 0.0), jnp.where(same_chunk, 1.0, 0.0)],
                              axis=0).astype(BF16)
    wf = wf_ref[...]
    bf = bf_ref[...]
    gn = gn_ref[...]
    st_ref[...] = jnp.zeros_like(st_ref)

    def tile(i, carry):
        rows = pl.ds(pl.multiple_of(i * t, t), t)
        q = q_ref[0, rows, :].astype(F32) * (GLA_HK ** -0.5)
        k = k_ref[0, rows, :].astype(F32)
        v = v_ref[0, rows, :]
        z = _dot(glr_ref[0, rows, :], wf) + bf
        log_a = (jnp.minimum(z, 0.0) - jnp.log1p(jnp.exp(-jnp.abs(z)))) / GATE_TAU
        la_hi = log_a.astype(BF16)
        la_lo = (log_a - la_hi.astype(F32)).astype(BF16)
        sums = _dot(sum_mat, la_hi) + _dot(sum_mat, la_lo)
        bcum = sums[:t]
        b_last = sums[t:]
        q_dec = (q * jnp.exp(bcum)).astype(BF16)
        k_inv = (k * jnp.exp(-bcum)).astype(BF16)
        k_end = (k * jnp.exp(b_last - bcum)).astype(BF16)
        decay = jnp.exp(b_last)
        att = jnp.where(causal, _dot_nt(q_dec, k_inv), 0.0).astype(BF16)
        o_intra = _dot(att, v)
        gate = r_ref[0, rows, :].astype(F32)
        gate = gate * _sigmoid(gate)
        st = st_ref[...]
        outs = []
        for c in range(group):
            cr = slice(c * c_len, (c + 1) * c_len)
            outs.append(o_intra[cr] + _dot_nt(q_dec[cr], st.astype(BF16)))
            st = st * decay[c * c_len:c * c_len + 1] + _dot_tn(v[cr], k_end[cr])
        st_ref[...] = st
        o = jnp.concatenate(outs, axis=0)
        o_ref[0, rows, :] = (_rms(o, gn) * gate).astype(o_ref.dtype)
        return carry

    lax.fori_loop(0, n_tiles, tile, 0)


def _gla(proj, glr, wf, bfg, gn, *, group=4):
    b, s, _ = proj.shape
    hk, hv = GLA_HK, GLA_HV
    k_off = GLA_DK // hk
    v_off = 2 * GLA_DK // hv
    r_off = (2 * GLA_DK + GLA_DV) // hv
    return pl.pallas_call(
        functools.partial(_gla_kernel, group=group),
        grid=(b, GLA_HEADS),
        in_specs=[
            pl.BlockSpec((1, s, hk), lambda i, h: (i, 0, h)),
            pl.BlockSpec((1, s, hk), lambda i, h: (i, 0, k_off + h)),
            pl.BlockSpec((1, s, hv), lambda i, h: (i, 0, v_off + h)),
            pl.BlockSpec((1, s, hv), lambda i, h: (i, 0, r_off + h)),
            pl.BlockSpec((1, s, LANES), lambda i, h: (i, 0, 0)),
            pl.BlockSpec((LANES, hk), lambda i, h: (0, h)),
            pl.BlockSpec((1, hk), lambda i, h: (0, h)),
            pl.BlockSpec((1, hv), lambda i, h: (0, 0)),
        ],
        out_specs=pl.BlockSpec((1, s, hv), lambda i, h: (i, 0, h)),
        out_shape=jax.ShapeDtypeStruct((b, s, GLA_DV), BF16),
        scratch_shapes=[pltpu.VMEM((hv, hk), F32)],
        compiler_params=_params("parallel", "parallel"),
        name="gla_mixer",
    )(proj, proj, proj, proj, glr, wf, bfg, gn)


def _out_proj_kernel(a_ref, w_ref, g_ref, x_ref, o_ref):
    mix = _dot(a_ref[...], w_ref[...])
    o_ref[...] = x_ref[...] + _rms(mix, g_ref[...])


def _out_proj(a, w, g, x, *, tm=512, name):
    m, k = a.shape
    n = w.shape[1]
    return pl.pallas_call(
        _out_proj_kernel,
        grid=(m // tm,),
        in_specs=[
            pl.BlockSpec((tm, k), lambda i: (i, 0)),
            pl.BlockSpec((k, n), lambda i: (0, 0)),
            pl.BlockSpec((1, n), lambda i: (0, 0)),
            pl.BlockSpec((tm, n), lambda i: (i, 0)),
        ],
        out_specs=pl.BlockSpec((tm, n), lambda i: (i, 0)),
        out_shape=jax.ShapeDtypeStruct((m, n), F32),
        compiler_params=_params("parallel"),
        name=name,
    )(a, w, g.reshape(1, n), x)


def _ffn_kernel(x_ref, gpre_ref, wg_ref, wu_ref, wd_ref, gpost_ref, o_ref, hn_ref):
    f = pl.program_id(1)

    @pl.when(f == 0)
    def _():
        hn_ref[...] = _rms(x_ref[...], gpre_ref[...]).astype(BF16)
        o_ref[...] = jnp.zeros_like(o_ref)

    h = hn_ref[...]
    gate = _dot(h, wg_ref[...])
    up = _dot(h, wu_ref[...])
    act = (gate * _sigmoid(gate) * up).astype(BF16)
    o_ref[...] += _dot(act, wd_ref[...])

    @pl.when(f == pl.num_programs(1) - 1)
    def _():
        o_ref[...] = x_ref[...] + _rms(o_ref[...], gpost_ref[...])


def _ffn(x, g_pre, w_gate_up, w_down, g_post, *, layer, tm=1024, tf=256, name):
    m, d = x.shape
    dff = w_down.shape[1]
    nf = dff // tf
    return pl.pallas_call(
        _ffn_kernel,
        grid=(m // tm, nf),
        in_specs=[
            pl.BlockSpec((tm, d), lambda i, f: (i, 0)),
            pl.BlockSpec((1, d), lambda i, f: (0, 0)),
            pl.BlockSpec((None, d, tf), lambda i, f: (layer, 0, f)),
            pl.BlockSpec((None, d, tf), lambda i, f: (layer, 0, nf + f)),
            pl.BlockSpec((None, tf, d), lambda i, f: (layer, f, 0)),
            pl.BlockSpec((1, d), lambda i, f: (0, 0)),
        ],
        out_specs=pl.BlockSpec((tm, d), lambda i, f: (i, 0)),
        out_shape=jax.ShapeDtypeStruct((m, d), F32),
        scratch_shapes=[pltpu.VMEM((tm, d), BF16)],
        compiler_params=_params("parallel", "arbitrary"),
        name=name,
    )(x, g_pre.reshape(1, d), w_gate_up, w_gate_up, w_down, g_post.reshape(1, d))


def _t5_bias_tile(dist, table_ref, h):
    n = jnp.maximum(dist, 0)
    nf = jnp.maximum(n, 1).astype(F32)
    large = REL_MAX_EXACT + (jnp.log(nf / REL_MAX_EXACT) / math.log(REL_MAX_DIST / REL_MAX_EXACT)
                             * (REL_BUCKETS - REL_MAX_EXACT)).astype(jnp.int32)
    large = jnp.minimum(large, REL_BUCKETS - 1)
    bucket = jnp.where(n < REL_MAX_EXACT, n, large)
    far = table_ref[REL_BUCKETS - 1, h]
    bias = jnp.zeros(dist.shape, F32)
    for b in range(REL_BUCKETS - 1):
        bias = jnp.where(bucket == b, table_ref[b, h] - far, bias)
    return jnp.where(dist >= 0, bias, MASK_VALUE)


def _attn_kernel(table_ref, lq1_ref, lk1_ref, lq2_ref, lk2_ref, q_ref, k_ref, v_ref, g_ref,
                 o_ref, bias_ref, *, lambda_init, tq):
    h = pl.program_id(0)
    b = pl.program_id(1)
    hd = DIFF_HD
    n_q = q_ref.shape[1] // tq

    @pl.when(b == 0)
    def _():
        rel = (lax.broadcasted_iota(jnp.int32, (tq, tq), 0)
               - lax.broadcasted_iota(jnp.int32, (tq, tq), 1))
        for r in range(2):
            bias_ref[r] = _t5_bias_tile(rel + r * tq, table_ref, h)

    lam = (jnp.exp(jnp.sum(lq1_ref[...] * lk1_ref[...], axis=-1, keepdims=True))
           - jnp.exp(jnp.sum(lq2_ref[...] * lk2_ref[...], axis=-1, keepdims=True))
           + lambda_init)
    g = g_ref[...]

    for qi in range(n_q):
        q_rows = slice(qi * tq, (qi + 1) * tq)
        segs = []
        if qi >= 2:
            segs.append((slice(0, (qi - 1) * tq), None))
        if qi >= 1:
            segs.append((slice((qi - 1) * tq, qi * tq), 1))
        segs.append((q_rows, 0))

        probs = []
        for br in range(2):
            cols = slice(br * hd, (br + 1) * hd)
            qb = q_ref[0, q_rows, cols]
            s = []
            for k_rows, r in segs:
                sc = _dot_nt(qb, k_ref[0, k_rows, cols])
                s.append(sc if r is None else sc + bias_ref[r])
            m = functools.reduce(jnp.maximum, [jnp.max(sc, axis=-1, keepdims=True) for sc in s])
            p = [jnp.exp(sc - m) for sc in s]
            l = functools.reduce(jnp.add, [jnp.sum(pc, axis=-1, keepdims=True) for pc in p])
            probs.append((p, l))

        c1 = 1.0 / probs[0][1]
        c2 = lam / probs[1][1]
        o = None
        for idx, (k_rows, _) in enumerate(segs):
            a = (probs[0][0][idx] * c1 - probs[1][0][idx] * c2).astype(BF16)
            part = _dot(a, v_ref[0, k_rows, :])
            o = part if o is None else o + part
        o_ref[0, q_rows, :] = (_rms(o, g) * (1.0 - lambda_init)).astype(o_ref.dtype)


def _attention(table, lq1, lk1, lq2, lk2, qkv, g, *, lambda_init, tq=256):
    b, s, _ = qkv.shape
    hw = 2 * DIFF_HD
    k_off = D_MODEL // hw
    v_off = 2 * k_off
    vec = pl.BlockSpec((1, DIFF_HD), lambda h, i: (0, 0))
    return pl.pallas_call(
        functools.partial(_attn_kernel, lambda_init=lambda_init, tq=tq),
        grid=(DIFF_HEADS, b),
        in_specs=[
            pl.BlockSpec(memory_space=pltpu.SMEM),
            vec, vec, vec, vec,
            pl.BlockSpec((1, s, hw), lambda h, i: (i, 0, h)),
            pl.BlockSpec((1, s, hw), lambda h, i: (i, 0, k_off + h)),
            pl.BlockSpec((1, s, hw), lambda h, i: (i, 0, v_off + h)),
            pl.BlockSpec((1, hw), lambda h, i: (0, 0)),
        ],
        out_specs=pl.BlockSpec((1, s, hw), lambda h, i: (i, 0, h)),
        out_shape=jax.ShapeDtypeStruct((b, s, D_MODEL), BF16),
        scratch_shapes=[pltpu.VMEM((2, tq, tq), F32)],
        compiler_params=_params("arbitrary", "arbitrary"),
        name="diff_attention",
    )(table, lq1.reshape(1, -1), lk1.reshape(1, -1), lq2.reshape(1, -1), lk2.reshape(1, -1),
      qkv, qkv, qkv, g.reshape(1, hw))


def kernel(x, rel_bias_table, kv_norm_g, w_kv, gla_w_in, gla_w_fgate, gla_b_fgate, gla_norm_g, gla_w_out, diff_w_q, diff_lam_q1, diff_lam_k1, diff_lam_q2, diff_lam_k2, diff_subln_g, diff_w_out, pre_mix_g, post_mix_g, pre_ffn_g, post_ffn_g, ffn_w_gate_up, ffn_w_down):
    b, s, d = x.shape
    m = b * s
    xf = x.reshape(m, d)

    w_in = gla_w_in[0]
    w_main = w_in[:, :GLA_MAIN].astype(BF16)
    w_glr = jnp.pad(w_in[:, GLA_MAIN:], ((0, 0), (0, LANES - GATE_RANK))).astype(BF16)
    proj, glr = _norm_matmul(xf, pre_mix_g[0], w_main, w_glr, name="gla_in_proj")
    wf = jnp.pad(gla_w_fgate[0], ((0, LANES - GATE_RANK), (0, 0))).astype(BF16)
    o = _gla(proj.reshape(b, s, GLA_MAIN), glr.reshape(b, s, LANES), wf,
             gla_b_fgate[0].reshape(1, GLA_DK), gla_norm_g[0].reshape(1, GLA_HV))
    xf = _out_proj(o.reshape(m, d), gla_w_out[0].astype(BF16), post_mix_g[0], xf,
                   name="gla_out_proj")
    w_gate_up = ffn_w_gate_up.astype(BF16)
    w_down = ffn_w_down.astype(BF16)
    xf = _ffn(xf, pre_ffn_g[0], w_gate_up, w_down, post_ffn_g[0], layer=0, name="ffn0")

    i = N_A_LAYERS
    lambda_init = 0.8 - 0.6 * math.exp(-0.3 * i)
    w_qkv = jnp.concatenate([diff_w_q[0], w_kv], axis=1).astype(BF16)
    gains = jnp.stack([pre_mix_g[i], kv_norm_g])
    qkv = _norm_matmul(xf, gains, w_qkv, scale=DIFF_HD ** -0.5, split=d, name="diff_qkv_proj")
    o = _attention(rel_bias_table, diff_lam_q1[0], diff_lam_k1[0], diff_lam_q2[0], diff_lam_k2[0],
                   qkv.reshape(b, s, 3 * d), diff_subln_g[0], lambda_init=lambda_init)
    xf = _out_proj(o.reshape(m, d), diff_w_out[0].astype(BF16), post_mix_g[i], xf,
                   name="diff_out_proj")
    xf = _ffn(xf, pre_ffn_g[i], w_gate_up, w_down, post_ffn_g[i], layer=i, name="ffn1")
    return xf.reshape(b, s, d)
```

```python
import functools
import math

import jax
import jax.numpy as jnp
from jax import lax
from jax.experimental import pallas as pl
from jax.experimental.pallas import tpu as pltpu

F32 = jnp.float32
BF16 = jnp.bfloat16

D_MODEL = 2048
DEPTH = 2
N_A_LAYERS = DEPTH // 2
GLA_HEADS = 4
GLA_DK = D_MODEL // 2
GLA_DV = D_MODEL
GLA_HK = GLA_DK // GLA_HEADS
GLA_HV = GLA_DV // GLA_HEADS
GATE_RANK = 16
GATE_TAU = 16.0
GLA_CHUNK = 64
GLA_MAIN = 2 * GLA_DK + 2 * GLA_DV
DIFF_HEADS = 8
DIFF_HD = D_MODEL // DIFF_HEADS // 2
REL_BUCKETS = 32
REL_MAX_EXACT = REL_BUCKETS // 2
REL_MAX_DIST = 128
D_FF = ((8 * D_MODEL // 3 + 255) // 256) * 256
EPS = 1e-6

LANES = 128
MASK_VALUE = -1e30
VMEM_LIMIT = 56 * 1024 * 1024


def _rms(x, g):
    y = x * lax.rsqrt(jnp.mean(x * x, axis=-1, keepdims=True) + EPS)
    return y * g


def _sigmoid(x):
    return 1.0 / (1.0 + jnp.exp(-x))


def _dot(a, b):
    return jnp.dot(a, b, preferred_element_type=F32)


def _dot_nt(a, b):
    return lax.dot_general(a, b, (((1,), (1,)), ((), ())), preferred_element_type=F32)


def _dot_tn(a, b):
    return lax.dot_general(a, b, (((0,), (0,)), ((), ())), preferred_element_type=F32)


def _params(*sem):
    return pltpu.CompilerParams(dimension_semantics=sem, vmem_limit_bytes=VMEM_LIMIT)


def _norm_matmul_kernel(x_ref, g_ref, w_ref, *rest, scale, split, has_aux):
    if has_aux:
        wa_ref, o_ref, oa_ref, hn_ref = rest
    else:
        o_ref, hn_ref = rest
    j = pl.program_id(1)

    @pl.when((j == 0) | (j == split))
    def _():
        hn_ref[...] = _rms(x_ref[...], g_ref[...]).astype(BF16)

    if has_aux:
        @pl.when(j == 0)
        def _():
            oa_ref[...] = _dot(hn_ref[...], wa_ref[...]).astype(oa_ref.dtype)

    acc = _dot(hn_ref[...], w_ref[...])
    if scale != 1.0:
        acc = acc * jnp.where(j < split, scale, 1.0)
    o_ref[...] = acc.astype(o_ref.dtype)


def _norm_matmul(x, g, w, w_aux=None, *, scale=1.0, split=None, tm=1024, tn=2048, name):
    m, k = x.shape
    n = w.shape[1]
    has_aux = w_aux is not None
    n_tiles = n // tn
    split = n_tiles if split is None else split // tn
    in_specs = [
        pl.BlockSpec((tm, k), lambda i, j: (i, 0)),
        pl.BlockSpec((None, 1, k), lambda i, j: (jnp.where(j < split, 0, 1), 0, 0)),
        pl.BlockSpec((k, tn), lambda i, j: (0, j)),
    ]
    out_shape = [jax.ShapeDtypeStruct((m, n), BF16)]
    out_specs = [pl.BlockSpec((tm, tn), lambda i, j: (i, j))]
    args = [x, g.reshape(-1, 1, k), w]
    if has_aux:
        na = w_aux.shape[1]
        in_specs.append(pl.BlockSpec((k, na), lambda i, j: (0, 0)))
        out_shape.append(jax.ShapeDtypeStruct((m, na), BF16))
        out_specs.append(pl.BlockSpec((tm, na), lambda i, j: (i, 0)))
        args.append(w_aux)
    res = pl.pallas_call(
        functools.partial(_norm_matmul_kernel, scale=scale, split=split, has_aux=has_aux),
        grid=(m // tm, n // tn),
        in_specs=in_specs,
        out_specs=out_specs,
        out_shape=out_shape,
        scratch_shapes=[pltpu.VMEM((tm, k), BF16)],
        compiler_params=_params("parallel", "arbitrary"),
        name=name,
    )(*args)
    return res if has_aux else res[0]


def _gla_kernel(q_ref, k_ref, v_ref, r_ref, glr_ref, wf_ref, bf_ref, gn_ref, o_ref, st_ref, *,
                group):
    c_len = GLA_CHUNK
    t = group * c_len
    n_tiles = q_ref.shape[1] // t
    row = lax.broadcasted_iota(jnp.int32, (t, t), 0)
    col = lax.broadcasted_iota(jnp.int32, (t, t), 1)
    same_chunk = (row // c_len) == (col // c_len)
    causal = same_chunk & (row >= col)
    sum_mat = jnp.concatenate([jnp.where(causal, 1.0, 0.0), jnp.where(same_chunk, 1.0, 0.0)],
                              axis=0).astype(BF16)
    wf = wf_ref[...]
    bf = bf_ref[...]
    gn = gn_ref[...]
    st_ref[...] = jnp.zeros_like(st_ref)

    def tile(i, carry):
        rows = pl.ds(pl.multiple_of(i * t, t), t)
        q = q_ref[0, rows, :].astype(F32) * (GLA_HK ** -0.5)
        k = k_ref[0, rows, :].astype(F32)
        v = v_ref[0, rows, :]
        z = _dot(glr_ref[0, rows, :], wf) + bf
        log_a = (jnp.minimum(z, 0.0) - jnp.log1p(jnp.exp(-jnp.abs(z)))) / GATE_TAU
        la_hi = log_a.astype(BF16)
        la_lo = (log_a - la_hi.astype(F32)).astype(BF16)
        sums = _dot(sum_mat, la_hi) + _dot(sum_mat, la_lo)
        bcum = sums[:t]
        b_last = sums[t:]
        q_dec = (q * jnp.exp(bcum)).astype(BF16)
        k_inv = (k * jnp.exp(-bcum)).astype(BF16)
        k_end = (k * jnp.exp(b_last - bcum)).astype(BF16)
        decay = jnp.exp(b_last)
        att = jnp.where(causal, _dot_nt(q_dec, k_inv), 0.0).astype(BF16)
        o_intra = _dot(att, v)
        gate = r_ref[0, rows, :].astype(F32)
        gate = gate * _sigmoid(gate)
        st = st_ref[...]
        outs = []
        for c in range(group):
            cr = slice(c * c_len, (c + 1) * c_len)
            outs.append(o_intra[cr] + _dot_nt(q_dec[cr], st.astype(BF16)))
            st = st * decay[c * c_len:c * c_len + 1] + _dot_tn(v[cr], k_end[cr])
        st_ref[...] = st
        o = jnp.concatenate(outs, axis=0)
        o_ref[0, rows, :] = (_rms(o, gn) * gate).astype(o_ref.dtype)
        return carry

    lax.fori_loop(0, n_tiles, tile, 0)


def _gla(proj, glr, wf, bfg, gn, *, group=4):
    b, s, _ = proj.shape
    hk, hv = GLA_HK, GLA_HV
    k_off = GLA_DK // hk
    v_off = 2 * GLA_DK // hv
    r_off = (2 * GLA_DK + GLA_DV) // hv
    return pl.pallas_call(
        functools.partial(_gla_kernel, group=group),
        grid=(b, GLA_HEADS),
        in_specs=[
            pl.BlockSpec((1, s, hk), lambda i, h: (i, 0, h)),
            pl.BlockSpec((1, s, hk), lambda i, h: (i, 0, k_off + h)),
            pl.BlockSpec((1, s, hv), lambda i, h: (i, 0, v_off + h)),
            pl.BlockSpec((1, s, hv), lambda i, h: (i, 0, r_off + h)),
            pl.BlockSpec((1, s, LANES), lambda i, h: (i, 0, 0)),
            pl.BlockSpec((LANES, hk), lambda i, h: (0, h)),
            pl.BlockSpec((1, hk), lambda i, h: (0, h)),
            pl.BlockSpec((1, hv), lambda i, h: (0, 0)),
        ],
        out_specs=pl.BlockSpec((1, s, hv), lambda i, h: (i, 0, h)),
        out_shape=jax.ShapeDtypeStruct((b, s, GLA_DV), BF16),
        scratch_shapes=[pltpu.VMEM((hv, hk), F32)],
        compiler_params=_params("parallel", "parallel"),
        name="gla_mixer",
    )(proj, proj, proj, proj, glr, wf, bfg, gn)


def _out_proj_kernel(a_ref, w_ref, g_ref, x_ref, o_ref):
    mix = _dot(a_ref[...], w_ref[...])
    o_ref[...] = x_ref[...] + _rms(mix, g_ref[...])


def _out_proj(a, w, g, x, *, tm=512, name):
    m, k = a.shape
    n = w.shape[1]
    return pl.pallas_call(
        _out_proj_kernel,
        grid=(m // tm,),
        in_specs=[
            pl.BlockSpec((tm, k), lambda i: (i, 0)),
            pl.BlockSpec((k, n), lambda i: (0, 0)),
            pl.BlockSpec((1, n), lambda i: (0, 0)),
            pl.BlockSpec((tm, n), lambda i: (i, 0)),
        ],
        out_specs=pl.BlockSpec((tm, n), lambda i: (i, 0)),
        out_shape=jax.ShapeDtypeStruct((m, n), F32),
        compiler_params=_params("parallel"),
        name=name,
    )(a, w, g.reshape(1, n), x)


def _ffn_kernel(x_ref, gpre_ref, wg_ref, wu_ref, wd_ref, gpost_ref, o_ref, hn_ref):
    f = pl.program_id(1)

    @pl.when(f == 0)
    def _():
        hn_ref[...] = _rms(x_ref[...], gpre_ref[...]).astype(BF16)
        o_ref[...] = jnp.zeros_like(o_ref)

    h = hn_ref[...]
    gate = _dot(h, wg_ref[...])
    up = _dot(h, wu_ref[...])
    act = (gate * _sigmoid(gate) * up).astype(BF16)
    o_ref[...] += _dot(act, wd_ref[...])

    @pl.when(f == pl.num_programs(1) - 1)
    def _():
        o_ref[...] = x_ref[...] + _rms(o_ref[...], gpost_ref[...])


def _ffn(x, g_pre, w_gate_up, w_down, g_post, *, layer, tm=1024, tf=512, name):
    m, d = x.shape
    dff = w_down.shape[1]
    nf = dff // tf
    return pl.pallas_call(
        _ffn_kernel,
        grid=(m // tm, nf),
        in_specs=[
            pl.BlockSpec((tm, d), lambda i, f: (i, 0), pipeline_mode=pl.Buffered(1)),
            pl.BlockSpec((1, d), lambda i, f: (0, 0)),
            pl.BlockSpec((None, d, tf), lambda i, f: (layer, 0, f)),
            pl.BlockSpec((None, d, tf), lambda i, f: (layer, 0, nf + f)),
            pl.BlockSpec((None, tf, d), lambda i, f: (layer, f, 0)),
            pl.BlockSpec((1, d), lambda i, f: (0, 0)),
        ],
        out_specs=pl.BlockSpec((tm, d), lambda i, f: (i, 0)),
        out_shape=jax.ShapeDtypeStruct((m, d), F32),
        scratch_shapes=[pltpu.VMEM((tm, d), BF16)],
        compiler_params=_params("parallel", "arbitrary"),
        name=name,
    )(x, g_pre.reshape(1, d), w_gate_up, w_gate_up, w_down, g_post.reshape(1, d))


def _t5_bias_tile(dist, table_ref, h):
    n = jnp.maximum(dist, 0)
    nf = jnp.maximum(n, 1).astype(F32)
    large = REL_MAX_EXACT + (jnp.log(nf / REL_MAX_EXACT) / math.log(REL_MAX_DIST / REL_MAX_EXACT)
                             * (REL_BUCKETS - REL_MAX_EXACT)).astype(jnp.int32)
    large = jnp.minimum(large, REL_BUCKETS - 1)
    bucket = jnp.where(n < REL_MAX_EXACT, n, large)
    far = table_ref[REL_BUCKETS - 1, h]
    bias = jnp.zeros(dist.shape, F32)
    for b in range(REL_BUCKETS - 1):
        bias = jnp.where(bucket == b, table_ref[b, h] - far, bias)
    return jnp.where(dist >= 0, bias, MASK_VALUE)


def _attn_kernel(table_ref, lq1_ref, lk1_ref, lq2_ref, lk2_ref, q_ref, k_ref, v_ref, g_ref,
                 o_ref, bias_ref, *, lambda_init, tq):
    h = pl.program_id(0)
    b = pl.program_id(1)
    hd = DIFF_HD
    n_q = q_ref.shape[1] // tq

    @pl.when(b == 0)
    def _():
        rel = (lax.broadcasted_iota(jnp.int32, (tq, tq), 0)
               - lax.broadcasted_iota(jnp.int32, (tq, tq), 1))
        for r in range(2):
            bias_ref[r] = _t5_bias_tile(rel + r * tq, table_ref, h)

    lam = (jnp.exp(jnp.sum(lq1_ref[...] * lk1_ref[...], axis=-1, keepdims=True))
           - jnp.exp(jnp.sum(lq2_ref[...] * lk2_ref[...], axis=-1, keepdims=True))
           + lambda_init)
    g = g_ref[...]

    for qi in range(n_q):
        q_rows = slice(qi * tq, (qi + 1) * tq)
        segs = []
        if qi >= 2:
            segs.append((slice(0, (qi - 1) * tq), None))
        if qi >= 1:
            segs.append((slice((qi - 1) * tq, qi * tq), 1))
        segs.append((q_rows, 0))

        probs = []
        for br in range(2):
            cols = slice(br * hd, (br + 1) * hd)
            qb = q_ref[0, q_rows, cols]
            s = []
            for k_rows, r in segs:
                sc = _dot_nt(qb, k_ref[0, k_rows, cols])
                s.append(sc if r is None else sc + bias_ref[r])
            m = functools.reduce(jnp.maximum, [jnp.max(sc, axis=-1, keepdims=True) for sc in s])
            p = [jnp.exp(sc - m) for sc in s]
            l = functools.reduce(jnp.add, [jnp.sum(pc, axis=-1, keepdims=True) for pc in p])
            probs.append((p, l))

        c1 = 1.0 / probs[0][1]
        c2 = lam / probs[1][1]
        o = None
        for idx, (k_rows, _) in enumerate(segs):
            a = (probs[0][0][idx] * c1 - probs[1][0][idx] * c2).astype(BF16)
            part = _dot(a, v_ref[0, k_rows, :])
            o = part if o is None else o + part
        o_ref[0, q_rows, :] = (_rms(o, g) * (1.0 - lambda_init)).astype(o_ref.dtype)


def _attention(table, lq1, lk1, lq2, lk2, qkv, g, *, lambda_init, tq=256):
    b, s, _ = qkv.shape
    hw = 2 * DIFF_HD
    k_off = D_MODEL // hw
    v_off = 2 * k_off
    vec = pl.BlockSpec((1, DIFF_HD), lambda h, i: (0, 0))
    return pl.pallas_call(
        functools.partial(_attn_kernel, lambda_init=lambda_init, tq=tq),
        grid=(DIFF_HEADS, b),
        in_specs=[
            pl.BlockSpec(memory_space=pltpu.SMEM),
            vec, vec, vec, vec,
            pl.BlockSpec((1, s, hw), lambda h, i: (i, 0, h)),
            pl.BlockSpec((1, s, hw), lambda h, i: (i, 0, k_off + h)),
            pl.BlockSpec((1, s, hw), lambda h, i: (i, 0, v_off + h)),
            pl.BlockSpec((1, hw), lambda h, i: (0, 0)),
        ],
        out_specs=pl.BlockSpec((1, s, hw), lambda h, i: (i, 0, h)),
        out_shape=jax.ShapeDtypeStruct((b, s, D_MODEL), BF16),
        scratch_shapes=[pltpu.VMEM((2, tq, tq), F32)],
        compiler_params=_params("arbitrary", "arbitrary"),
        name="diff_attention",
    )(table, lq1.reshape(1, -1), lk1.reshape(1, -1), lq2.reshape(1, -1), lk2.reshape(1, -1),
      qkv, qkv, qkv, g.reshape(1, hw))


def kernel(x, rel_bias_table, kv_norm_g, w_kv, gla_w_in, gla_w_fgate, gla_b_fgate, gla_norm_g, gla_w_out, diff_w_q, diff_lam_q1, diff_lam_k1, diff_lam_q2, diff_lam_k2, diff_subln_g, diff_w_out, pre_mix_g, post_mix_g, pre_ffn_g, post_ffn_g, ffn_w_gate_up, ffn_w_down):
    b, s, d = x.shape
    m = b * s
    xf = x.reshape(m, d)

    w_in = gla_w_in[0]
    w_main = w_in[:, :GLA_MAIN].astype(BF16)
    w_glr = jnp.pad(w_in[:, GLA_MAIN:], ((0, 0), (0, LANES - GATE_RANK))).astype(BF16)
    proj, glr = _norm_matmul(xf, pre_mix_g[0], w_main, w_glr, name="gla_in_proj")
    wf = jnp.pad(gla_w_fgate[0], ((0, LANES - GATE_RANK), (0, 0))).astype(BF16)
    o = _gla(proj.reshape(b, s, GLA_MAIN), glr.reshape(b, s, LANES), wf,
             gla_b_fgate[0].reshape(1, GLA_DK), gla_norm_g[0].reshape(1, GLA_HV))
    xf = _out_proj(o.reshape(m, d), gla_w_out[0].astype(BF16), post_mix_g[0], xf,
                   name="gla_out_proj")
    w_gate_up = ffn_w_gate_up.astype(BF16)
    w_down = ffn_w_down.astype(BF16)
    xf = _ffn(xf, pre_ffn_g[0], w_gate_up, w_down, post_ffn_g[0], layer=0, name="ffn0")

    i = N_A_LAYERS
    lambda_init = 0.8 - 0.6 * math.exp(-0.3 * i)
    w_qkv = jnp.concatenate([diff_w_q[0], w_kv], axis=1).astype(BF16)
    gains = jnp.stack([pre_mix_g[i], kv_norm_g])
    qkv = _norm_matmul(xf, gains, w_qkv, scale=DIFF_HD ** -0.5, split=d, name="diff_qkv_proj")
    o = _attention(rel_bias_table, diff_lam_q1[0], diff_lam_k1[0], diff_lam_q2[0], diff_lam_k2[0],
                   qkv.reshape(b, s, 3 * d), diff_subln_g[0], lambda_init=lambda_init)
    xf = _out_proj(o.reshape(m, d), diff_w_out[0].astype(BF16), post_mix_g[i], xf,
                   name="diff_out_proj")
    xf = _ffn(xf, pre_ffn_g[i], w_gate_up, w_down, post_ffn_g[i], layer=i, name="ffn1")
    return xf.reshape(b, s, d)
```

```python
import functools
import math

import jax
import jax.numpy as jnp
from jax import lax
from jax.experimental import pallas as pl
from jax.experimental.pallas import tpu as pltpu

F32 = jnp.float32
BF16 = jnp.bfloat16

D_MODEL = 2048
DEPTH = 2
N_A_LAYERS = DEPTH // 2
GLA_HEADS = 4
GLA_DK = D_MODEL // 2
GLA_DV = D_MODEL
GLA_HK = GLA_DK // GLA_HEADS
GLA_HV = GLA_DV // GLA_HEADS
GATE_RANK = 16
GATE_TAU = 16.0
GLA_CHUNK = 64
GLA_MAIN = 2 * GLA_DK + 2 * GLA_DV
DIFF_HEADS = 8
DIFF_HD = D_MODEL // DIFF_HEADS // 2
REL_BUCKETS = 32
REL_MAX_EXACT = REL_BUCKETS // 2
REL_MAX_DIST = 128
D_FF = ((8 * D_MODEL // 3 + 255) // 256) * 256
EPS = 1e-6

LANES = 128
LOG2E = math.log2(math.e)
MASK_VALUE = -1e30
VMEM_LIMIT = 60 * 1024 * 1024


def _rms(x, g):
    y = x * lax.rsqrt(jnp.mean(x * x, axis=-1, keepdims=True) + EPS)
    return y * g


def _sigmoid(x):
    return 1.0 / (1.0 + jnp.exp(-x))


def _dot(a, b):
    return jnp.dot(a, b, preferred_element_type=F32)


def _dot_nt(a, b):
    return lax.dot_general(a, b, (((1,), (1,)), ((), ())), preferred_element_type=F32)


def _dot_tn(a, b):
    return lax.dot_general(a, b, (((0,), (0,)), ((), ())), preferred_element_type=F32)


def _params(*sem):
    return pltpu.CompilerParams(dimension_semantics=sem, vmem_limit_bytes=VMEM_LIMIT)


def _norm_matmul_kernel(x_ref, g_ref, w_ref, *rest, scale, split, has_aux):
    if has_aux:
        wa_ref, o_ref, oa_ref, hn_ref = rest
    else:
        o_ref, hn_ref = rest
    j = pl.program_id(1)

    @pl.when((j == 0) | (j == split))
    def _():
        hn_ref[...] = _rms(x_ref[...], g_ref[...]).astype(BF16)

    if has_aux:
        @pl.when(j == 0)
        def _():
            oa_ref[...] = _dot(hn_ref[...], wa_ref[...]).astype(oa_ref.dtype)

    acc = _dot(hn_ref[...], w_ref[...])
    if scale != 1.0:
        acc = acc * jnp.where(j < split, scale, 1.0)
    o_ref[...] = acc.astype(o_ref.dtype)


def _norm_matmul(x, g, w, w_aux=None, *, scale=1.0, split=None, n_out=None, tm=1024, tn=2048,
                 name):
    m, k = x.shape
    n = w.shape[1] if n_out is None else n_out
    has_aux = w_aux is not None
    n_tiles = n // tn
    split = n_tiles if split is None else split // tn
    in_specs = [
        pl.BlockSpec((tm, k), lambda i, j: (i, 0)),
        pl.BlockSpec((None, 1, k), lambda i, j: (jnp.where(j < split, 0, 1), 0, 0)),
        pl.BlockSpec((k, tn), lambda i, j: (0, j)),
    ]
    out_shape = [jax.ShapeDtypeStruct((m, n), BF16)]
    out_specs = [pl.BlockSpec((tm, tn), lambda i, j: (i, j))]
    args = [x, g.reshape(-1, 1, k), w]
    if has_aux:
        na = w_aux.shape[1]
        in_specs.append(pl.BlockSpec((k, na), lambda i, j: (0, 0)))
        out_shape.append(jax.ShapeDtypeStruct((m, na), BF16))
        out_specs.append(pl.BlockSpec((tm, na), lambda i, j: (i, 0)))
        args.append(w_aux)
    res = pl.pallas_call(
        functools.partial(_norm_matmul_kernel, scale=scale, split=split, has_aux=has_aux),
        grid=(m // tm, n // tn),
        in_specs=in_specs,
        out_specs=out_specs,
        out_shape=out_shape,
        scratch_shapes=[pltpu.VMEM((tm, k), BF16)],
        compiler_params=_params("parallel", "arbitrary"),
        name=name,
    )(*args)
    return res if has_aux else res[0]


def _gla_kernel(q_ref, k_ref, v_ref, r_ref, glr_ref, wf_ref, bf_ref, gn_ref, o_ref, *, group):
    c_len = GLA_CHUNK
    t = group * c_len
    n_tiles = q_ref.shape[1] // t
    row = lax.broadcasted_iota(jnp.int32, (t, t), 0)
    col = lax.broadcasted_iota(jnp.int32, (t, t), 1)
    same_chunk = (row // c_len) == (col // c_len)
    causal = same_chunk & (row >= col)
    sum_mat = jnp.concatenate([jnp.where(causal, 1.0, 0.0), jnp.where(same_chunk, 1.0, 0.0)],
                              axis=0).astype(BF16)
    wf = wf_ref[...]
    bf = bf_ref[...]
    gn = gn_ref[...]

    def local_part(i):
        rows = slice(i * t, (i + 1) * t)
        q = q_ref[0, rows, :].astype(F32) * (GLA_HK ** -0.5)
        k = k_ref[0, rows, :].astype(F32)
        v = v_ref[0, rows, :]
        z = _dot(glr_ref[0, rows, :], wf) + bf
        log_a = (jnp.minimum(z, 0.0) - jnp.log1p(jnp.exp(-jnp.abs(z)))) / GATE_TAU
        la_hi = log_a.astype(BF16)
        la_lo = (log_a - la_hi.astype(F32)).astype(BF16)
        sums = _dot(sum_mat, la_hi) + _dot(sum_mat, la_lo)
        bcum = sums[:t]
        b_last = sums[t:]
        q_dec = (q * jnp.exp(bcum)).astype(BF16)
        k_inv = (k * jnp.exp(-bcum)).astype(BF16)
        k_end = (k * jnp.exp(b_last - bcum)).astype(BF16)
        decay = jnp.exp(b_last)
        att = jnp.where(causal, _dot_nt(q_dec, k_inv), 0.0).astype(BF16)
        o_intra = _dot(att, v)
        gate = r_ref[0, rows, :].astype(F32)
        gate = gate * _sigmoid(gate)
        return v, q_dec, k_end, decay, o_intra, gate

    def state_part(i, st, v, q_dec, k_end, decay, o_intra, gate):
        outs = []
        for c in range(group):
            cr = slice(c * c_len, (c + 1) * c_len)
            outs.append(o_intra[cr] + _dot_nt(q_dec[cr], st.astype(BF16)))
            st = st * decay[c * c_len:c * c_len + 1] + _dot_tn(v[cr], k_end[cr])
        o = jnp.concatenate(outs, axis=0)
        o_ref[0, i * t:(i + 1) * t, :] = (_rms(o, gn) * gate).astype(o_ref.dtype)
        return st

    st = jnp.zeros((GLA_HV, GLA_HK), F32)
    nxt = local_part(0)
    for i in range(n_tiles):
        cur = nxt
        if i + 1 < n_tiles:
            nxt = local_part(i + 1)
        st = state_part(i, st, *cur)


def _gla(proj, glr, wf, bfg, gn, *, group=4):
    b, s, _ = proj.shape
    hk, hv = GLA_HK, GLA_HV
    k_off = GLA_DK // hk
    v_off = 2 * GLA_DK // hv
    r_off = (2 * GLA_DK + GLA_DV) // hv
    return pl.pallas_call(
        functools.partial(_gla_kernel, group=group),
        grid=(b, GLA_HEADS),
        in_specs=[
            pl.BlockSpec((1, s, hk), lambda i, h: (i, 0, h)),
            pl.BlockSpec((1, s, hk), lambda i, h: (i, 0, k_off + h)),
            pl.BlockSpec((1, s, hv), lambda i, h: (i, 0, v_off + h)),
            pl.BlockSpec((1, s, hv), lambda i, h: (i, 0, r_off + h)),
            pl.BlockSpec((1, s, LANES), lambda i, h: (i, 0, 0)),
            pl.BlockSpec((LANES, hk), lambda i, h: (0, h)),
            pl.BlockSpec((1, hk), lambda i, h: (0, h)),
            pl.BlockSpec((1, hv), lambda i, h: (0, 0)),
        ],
        out_specs=pl.BlockSpec((1, s, hv), lambda i, h: (i, 0, h)),
        out_shape=jax.ShapeDtypeStruct((b, s, GLA_DV), BF16),
        compiler_params=_params("parallel", "parallel"),
        name="gla_mixer",
    )(proj, proj, proj, proj, glr, wf, bfg, gn)


def _out_proj_kernel(a_ref, w_ref, g_ref, x_ref, o_ref):
    mix = _dot(a_ref[...], w_ref[...].astype(BF16))
    o_ref[...] = x_ref[...] + _rms(mix, g_ref[...])


def _out_proj(a, w, g, x, *, tm=512, name):
    m, k = a.shape
    n = w.shape[1]
    return pl.pallas_call(
        _out_proj_kernel,
        grid=(m // tm,),
        in_specs=[
            pl.BlockSpec((tm, k), lambda i: (i, 0)),
            pl.BlockSpec((k, n), lambda i: (0, 0), pipeline_mode=pl.Buffered(1)),
            pl.BlockSpec((1, n), lambda i: (0, 0)),
            pl.BlockSpec((tm, n), lambda i: (i, 0)),
        ],
        out_specs=pl.BlockSpec((tm, n), lambda i: (i, 0)),
        out_shape=jax.ShapeDtypeStruct((m, n), F32),
        compiler_params=_params("parallel"),
        name=name,
    )(a, w, g.reshape(1, n), x)


def _ffn_kernel(x_ref, gpre_ref, wg_ref, wu_ref, wd_ref, gpost_ref, o_ref, hn_ref):
    f = pl.program_id(1)

    @pl.when(f == 0)
    def _():
        hn_ref[...] = _rms(x_ref[...], gpre_ref[...]).astype(BF16)
        o_ref[...] = jnp.zeros_like(o_ref)

    h = hn_ref[...]
    gate = _dot(h, wg_ref[...].astype(BF16))
    up = _dot(h, wu_ref[...].astype(BF16))
    act = (gate * _sigmoid(gate) * up).astype(BF16)
    o_ref[...] += _dot(act, wd_ref[...].astype(BF16))

    @pl.when(f == pl.num_programs(1) - 1)
    def _():
        o_ref[...] = x_ref[...] + _rms(o_ref[...], gpost_ref[...])


def _ffn(x, g_pre, w_gate_up, w_down, g_post, *, layer, tm=1024, tf=256, name):
    m, d = x.shape
    dff = w_down.shape[1]
    nf = dff // tf
    return pl.pallas_call(
        _ffn_kernel,
        grid=(m // tm, nf),
        in_specs=[
            pl.BlockSpec((tm, d), lambda i, f: (i, 0)),
            pl.BlockSpec((1, d), lambda i, f: (0, 0)),
            pl.BlockSpec((None, d, tf), lambda i, f: (layer, 0, f)),
            pl.BlockSpec((None, d, tf), lambda i, f: (layer, 0, nf + f)),
            pl.BlockSpec((None, tf, d), lambda i, f: (layer, f, 0)),
            pl.BlockSpec((1, d), lambda i, f: (0, 0)),
        ],
        out_specs=pl.BlockSpec((tm, d), lambda i, f: (i, 0)),
        out_shape=jax.ShapeDtypeStruct((m, d), F32),
        scratch_shapes=[pltpu.VMEM((tm, d), BF16)],
        compiler_params=_params("parallel", "arbitrary"),
        name=name,
    )(x, g_pre.reshape(1, d), w_gate_up, w_gate_up, w_down, g_post.reshape(1, d))


def _t5_bias_tile(dist, table_ref, h):
    n = jnp.maximum(dist, 0)
    nf = jnp.maximum(n, 1).astype(F32)
    large = REL_MAX_EXACT + (jnp.log(nf / REL_MAX_EXACT) / math.log(REL_MAX_DIST / REL_MAX_EXACT)
                             * (REL_BUCKETS - REL_MAX_EXACT)).astype(jnp.int32)
    large = jnp.minimum(large, REL_BUCKETS - 1)
    bucket = jnp.where(n < REL_MAX_EXACT, n, large)
    far = table_ref[REL_BUCKETS - 1, h]
    bias = jnp.zeros(dist.shape, F32)
    for b in range(REL_BUCKETS - 1):
        bias = jnp.where(bucket == b, (table_ref[b, h] - far) * LOG2E, bias)
    return jnp.where(dist >= 0, bias, MASK_VALUE)


def _lane_chunks(xs):
    return [x[:, i:i + LANES] for x in xs for i in range(0, x.shape[1], LANES)]


def _attn_kernel(table_ref, lq1_ref, lk1_ref, lq2_ref, lk2_ref, q_ref, k_ref, v_ref, g_ref,
                 o_ref, bias_ref, *, lambda_init, tq):
    h = pl.program_id(0)
    b = pl.program_id(1)
    hd = DIFF_HD
    n_q = q_ref.shape[1] // tq

    @pl.when(b == 0)
    def _():
        rel = (lax.broadcasted_iota(jnp.int32, (tq, tq), 0)
               - lax.broadcasted_iota(jnp.int32, (tq, tq), 1))
        for r in range(2):
            bias_ref[r] = _t5_bias_tile(rel + r * tq, table_ref, h)

    lam = (jnp.exp(jnp.sum(lq1_ref[...] * lk1_ref[...], axis=-1, keepdims=True))
           - jnp.exp(jnp.sum(lq2_ref[...] * lk2_ref[...], axis=-1, keepdims=True))
           + lambda_init)
    g = g_ref[...]

    def segments(qi):
        segs = []
        if qi >= 2:
            segs.append((slice(0, (qi - 1) * tq), None))
        if qi >= 1:
            segs.append((slice((qi - 1) * tq, qi * tq), 1))
        segs.append((slice(qi * tq, (qi + 1) * tq), 0))
        return segs

    def scores(qi):
        q_rows = slice(qi * tq, (qi + 1) * tq)
        out = []
        for br in range(2):
            cols = slice(br * hd, (br + 1) * hd)
            qb = q_ref[0, q_rows, cols]
            s = []
            for k_rows, r in segments(qi):
                sc = _dot_nt(qb, k_ref[0, k_rows, cols])
                s.append(sc if r is None else sc + bias_ref[r])
            out.append(s)
        return out

    def weights(sc):
        probs = []
        for s in sc:
            m = jnp.max(functools.reduce(jnp.maximum, _lane_chunks(s)), axis=-1, keepdims=True)
            p = [jnp.exp2(x - m) for x in s]
            l = jnp.sum(functools.reduce(jnp.add, _lane_chunks(p)), axis=-1, keepdims=True)
            probs.append((p, l))
        c1 = 1.0 / probs[0][1]
        ratio = lam * probs[0][1] / probs[1][1]
        return [(p1 - p2 * ratio).astype(BF16) for p1, p2 in zip(probs[0][0], probs[1][0])], c1

    def output(qi, a, c1):
        o = None
        for (k_rows, _), ac in zip(segments(qi), a):
            part = _dot(ac, v_ref[0, k_rows, :])
            o = part if o is None else o + part
        o = o * c1
        o_ref[0, qi * tq:(qi + 1) * tq, :] = (_rms(o, g) * (1.0 - lambda_init)).astype(o_ref.dtype)

    sc_next = scores(0)
    pending = None
    for qi in range(n_q):
        sc = sc_next
        if qi + 1 < n_q:
            sc_next = scores(qi + 1)
        a = weights(sc)
        if pending is not None:
            output(*pending)
        pending = (qi,) + a
    output(*pending)


def _attention(table, lq1, lk1, lq2, lk2, qkv, g, *, lambda_init, tq=256):
    b, s, _ = qkv.shape
    hw = 2 * DIFF_HD
    k_off = D_MODEL // hw
    v_off = 2 * k_off
    vec = pl.BlockSpec((1, DIFF_HD), lambda h, i: (0, 0))
    return pl.pallas_call(
        functools.partial(_attn_kernel, lambda_init=lambda_init, tq=tq),
        grid=(DIFF_HEADS, b),
        in_specs=[
            pl.BlockSpec(memory_space=pltpu.SMEM),
            vec, vec, vec, vec,
            pl.BlockSpec((1, s, hw), lambda h, i: (i, 0, h)),
            pl.BlockSpec((1, s, hw), lambda h, i: (i, 0, k_off + h)),
            pl.BlockSpec((1, s, hw), lambda h, i: (i, 0, v_off + h)),
            pl.BlockSpec((1, hw), lambda h, i: (0, 0)),
        ],
        out_specs=pl.BlockSpec((1, s, hw), lambda h, i: (i, 0, h)),
        out_shape=jax.ShapeDtypeStruct((b, s, D_MODEL), BF16),
        scratch_shapes=[pltpu.VMEM((2, tq, tq), F32)],
        compiler_params=_params("arbitrary", "arbitrary"),
        name="diff_attention",
    )(table, lq1.reshape(1, -1), lk1.reshape(1, -1), lq2.reshape(1, -1), lk2.reshape(1, -1),
      qkv, qkv, qkv, g.reshape(1, hw))


def kernel(x, rel_bias_table, kv_norm_g, w_kv, gla_w_in, gla_w_fgate, gla_b_fgate, gla_norm_g, gla_w_out, diff_w_q, diff_lam_q1, diff_lam_k1, diff_lam_q2, diff_lam_k2, diff_subln_g, diff_w_out, pre_mix_g, post_mix_g, pre_ffn_g, post_ffn_g, ffn_w_gate_up, ffn_w_down):
    b, s, d = x.shape
    m = b * s
    xf = x.reshape(m, d)

    w_in = gla_w_in[0].astype(BF16)
    w_glr = jnp.pad(w_in[:, GLA_MAIN:], ((0, 0), (0, LANES - GATE_RANK)))
    proj, glr = _norm_matmul(xf, pre_mix_g[0], w_in, w_glr, n_out=GLA_MAIN, name="gla_in_proj")
    wf = jnp.pad(gla_w_fgate[0], ((0, LANES - GATE_RANK), (0, 0))).astype(BF16)
    o = _gla(proj.reshape(b, s, GLA_MAIN), glr.reshape(b, s, LANES), wf,
             gla_b_fgate[0].reshape(1, GLA_DK), gla_norm_g[0].reshape(1, GLA_HV))
    xf = _out_proj(o.reshape(m, d), gla_w_out[0], post_mix_g[0], xf, name="gla_out_proj")
    xf = _ffn(xf, pre_ffn_g[0], ffn_w_gate_up, ffn_w_down, post_ffn_g[0], layer=0, name="ffn0")

    i = N_A_LAYERS
    lambda_init = 0.8 - 0.6 * math.exp(-0.3 * i)
    w_qkv = jnp.concatenate([diff_w_q[0], w_kv], axis=1).astype(BF16)
    gains = jnp.stack([pre_mix_g[i], kv_norm_g])
    qkv = _norm_matmul(xf, gains, w_qkv, scale=DIFF_HD ** -0.5 * LOG2E, split=d,
                       name="diff_qkv_proj")
    o = _attention(rel_bias_table, diff_lam_q1[0], diff_lam_k1[0], diff_lam_q2[0], diff_lam_k2[0],
                   qkv.reshape(b, s, 3 * d), diff_subln_g[0], lambda_init=lambda_init)
    xf = _out_proj(o.reshape(m, d), diff_w_out[0], post_mix_g[i], xf, name="diff_out_proj")
    xf = _ffn(xf, pre_ffn_g[i], ffn_w_gate_up, ffn_w_down, post_ffn_g[i], layer=i, name="ffn1")
    return xf.reshape(b, s, d)
```

```python
import functools
import math

import jax
import jax.numpy as jnp
from jax import lax
from jax.experimental import pallas as pl
from jax.experimental.pallas import tpu as pltpu

F32 = jnp.float32
BF16 = jnp.bfloat16

D_MODEL = 2048
DEPTH = 2
N_A_LAYERS = DEPTH // 2
GLA_HEADS = 4
GLA_DK = D_MODEL // 2
GLA_DV = D_MODEL
GLA_HK = GLA_DK // GLA_HEADS
GLA_HV = GLA_DV // GLA_HEADS
GATE_RANK = 16
GATE_TAU = 16.0
GLA_CHUNK = 64
GLA_MAIN = 2 * GLA_DK + 2 * GLA_DV
DIFF_HEADS = 8
DIFF_HD = D_MODEL // DIFF_HEADS // 2
REL_BUCKETS = 32
REL_MAX_EXACT = REL_BUCKETS // 2
REL_MAX_DIST = 128
D_FF = ((8 * D_MODEL // 3 + 255) // 256) * 256
EPS = 1e-6

LANES = 128
LOG2E = math.log2(math.e)
MASK_VALUE = -1e30
VMEM_LIMIT = 60 * 1024 * 1024


def _rms(x, g):
    y = x * lax.rsqrt(jnp.mean(x * x, axis=-1, keepdims=True) + EPS)
    return y * g


def _sigmoid(x):
    return 1.0 / (1.0 + jnp.exp(-x))


def _dot(a, b):
    return jnp.dot(a, b, preferred_element_type=F32)


def _dot_nt(a, b):
    return lax.dot_general(a, b, (((1,), (1,)), ((), ())), preferred_element_type=F32)


def _dot_tn(a, b):
    return lax.dot_general(a, b, (((0,), (0,)), ((), ())), preferred_element_type=F32)


def _params(*sem):
    return pltpu.CompilerParams(dimension_semantics=sem, vmem_limit_bytes=VMEM_LIMIT)


def _norm_matmul_kernel(x_ref, g_ref, w_ref, *rest, scale, split, has_aux):
    if has_aux:
        wa_ref, o_ref, oa_ref, hn_ref = rest
    else:
        o_ref, hn_ref = rest
    j = pl.program_id(1)

    @pl.when((j == 0) | (j == split))
    def _():
        hn_ref[...] = _rms(x_ref[...], g_ref[...]).astype(BF16)

    if has_aux:
        @pl.when(j == 0)
        def _():
            oa_ref[...] = _dot(hn_ref[...], wa_ref[...]).astype(oa_ref.dtype)

    acc = _dot(hn_ref[...], w_ref[...])
    if scale != 1.0:
        acc = acc * jnp.where(j < split, scale, 1.0)
    o_ref[...] = acc.astype(o_ref.dtype)


def _norm_matmul(x, g, w, w_aux=None, *, scale=1.0, split=None, n_out=None, tm=1024, tn=2048,
                 name):
    m, k = x.shape
    n = w.shape[1] if n_out is None else n_out
    has_aux = w_aux is not None
    n_tiles = n // tn
    split = n_tiles if split is None else split // tn
    in_specs = [
        pl.BlockSpec((tm, k), lambda i, j: (i, 0)),
        pl.BlockSpec((None, 1, k), lambda i, j: (jnp.where(j < split, 0, 1), 0, 0)),
        pl.BlockSpec((k, tn), lambda i, j: (0, j)),
    ]
    out_shape = [jax.ShapeDtypeStruct((m, n), BF16)]
    out_specs = [pl.BlockSpec((tm, tn), lambda i, j: (i, j))]
    args = [x, g.reshape(-1, 1, k), w]
    if has_aux:
        na = w_aux.shape[1]
        in_specs.append(pl.BlockSpec((k, na), lambda i, j: (0, 0)))
        out_shape.append(jax.ShapeDtypeStruct((m, na), BF16))
        out_specs.append(pl.BlockSpec((tm, na), lambda i, j: (i, 0)))
        args.append(w_aux)
    res = pl.pallas_call(
        functools.partial(_norm_matmul_kernel, scale=scale, split=split, has_aux=has_aux),
        grid=(m // tm, n // tn),
        in_specs=in_specs,
        out_specs=out_specs,
        out_shape=out_shape,
        scratch_shapes=[pltpu.VMEM((tm, k), BF16)],
        compiler_params=_params("parallel", "arbitrary"),
        name=name,
    )(*args)
    return res if has_aux else res[0]


def _gla_kernel(q_ref, k_ref, v_ref, r_ref, glr_ref, wf_ref, bf_ref, gn_ref, o_ref, *, group):
    c_len = GLA_CHUNK
    t = group * c_len
    n_tiles = q_ref.shape[1] // t
    row = lax.broadcasted_iota(jnp.int32, (t, t), 0)
    col = lax.broadcasted_iota(jnp.int32, (t, t), 1)
    same_chunk = (row // c_len) == (col // c_len)
    causal = same_chunk & (row >= col)
    sum_mat = jnp.concatenate([jnp.where(causal, 1.0, 0.0), jnp.where(same_chunk, 1.0, 0.0)],
                              axis=0).astype(BF16)
    wf = wf_ref[...]
    bf = bf_ref[...]
    gn = gn_ref[...]

    def local_part(i):
        rows = slice(i * t, (i + 1) * t)
        q = q_ref[0, rows, :].astype(F32) * (GLA_HK ** -0.5)
        k = k_ref[0, rows, :].astype(F32)
        v = v_ref[0, rows, :]
        z = _dot(glr_ref[0, rows, :], wf) + bf
        log_a = (jnp.minimum(z, 0.0) - jnp.log1p(jnp.exp(-jnp.abs(z)))) / GATE_TAU
        la_hi = log_a.astype(BF16)
        la_lo = (log_a - la_hi.astype(F32)).astype(BF16)
        sums = _dot(sum_mat, la_hi) + _dot(sum_mat, la_lo)
        bcum = sums[:t]
        b_last = sums[t:]
        q_dec = (q * jnp.exp(bcum)).astype(BF16)
        k_inv = (k * jnp.exp(-bcum)).astype(BF16)
        k_end = (k * jnp.exp(b_last - bcum)).astype(BF16)
        decay = jnp.exp(b_last)
        att = jnp.where(causal, _dot_nt(q_dec, k_inv), 0.0).astype(BF16)
        o_intra = _dot(att, v)
        gate = r_ref[0, rows, :].astype(F32)
        gate = gate * _sigmoid(gate)
        return v, q_dec, k_end, decay, o_intra, gate

    def state_part(i, st, v, q_dec, k_end, decay, o_intra, gate):
        outs = []
        for c in range(group):
            cr = slice(c * c_len, (c + 1) * c_len)
            outs.append(o_intra[cr] + _dot_nt(q_dec[cr], st.astype(BF16)))
            st = st * decay[c * c_len:c * c_len + 1] + _dot_tn(v[cr], k_end[cr])
        o = jnp.concatenate(outs, axis=0)
        o_ref[0, i * t:(i + 1) * t, :] = (_rms(o, gn) * gate).astype(o_ref.dtype)
        return st

    st = jnp.zeros((GLA_HV, GLA_HK), F32)
    nxt = local_part(0)
    for i in range(n_tiles):
        cur = nxt
        if i + 1 < n_tiles:
            nxt = local_part(i + 1)
        st = state_part(i, st, *cur)


def _gla(proj, glr, wf, bfg, gn, *, group=4):
    b, s, _ = proj.shape
    hk, hv = GLA_HK, GLA_HV
    k_off = GLA_DK // hk
    v_off = 2 * GLA_DK // hv
    r_off = (2 * GLA_DK + GLA_DV) // hv
    return pl.pallas_call(
        functools.partial(_gla_kernel, group=group),
        grid=(b, GLA_HEADS),
        in_specs=[
            pl.BlockSpec((1, s, hk), lambda i, h: (i, 0, h)),
            pl.BlockSpec((1, s, hk), lambda i, h: (i, 0, k_off + h)),
            pl.BlockSpec((1, s, hv), lambda i, h: (i, 0, v_off + h)),
            pl.BlockSpec((1, s, hv), lambda i, h: (i, 0, r_off + h)),
            pl.BlockSpec((1, s, LANES), lambda i, h: (i, 0, 0)),
            pl.BlockSpec((LANES, hk), lambda i, h: (0, h)),
            pl.BlockSpec((1, hk), lambda i, h: (0, h)),
            pl.BlockSpec((1, hv), lambda i, h: (0, 0)),
        ],
        out_specs=pl.BlockSpec((1, s, hv), lambda i, h: (i, 0, h)),
        out_shape=jax.ShapeDtypeStruct((b, s, GLA_DV), BF16),
        compiler_params=_params("parallel", "parallel"),
        name="gla_mixer",
    )(proj, proj, proj, proj, glr, wf, bfg, gn)


def _out_proj_kernel(a_ref, w_ref, g_ref, x_ref, o_ref):
    mix = _dot(a_ref[...], w_ref[...].astype(BF16))
    o_ref[...] = x_ref[...] + _rms(mix, g_ref[...])


def _out_proj(a, w, g, x, *, tm=512, name):
    m, k = a.shape
    n = w.shape[1]
    return pl.pallas_call(
        _out_proj_kernel,
        grid=(m // tm,),
        in_specs=[
            pl.BlockSpec((tm, k), lambda i: (i, 0)),
            pl.BlockSpec((k, n), lambda i: (0, 0), pipeline_mode=pl.Buffered(1)),
            pl.BlockSpec((1, n), lambda i: (0, 0)),
            pl.BlockSpec((tm, n), lambda i: (i, 0)),
        ],
        out_specs=pl.BlockSpec((tm, n), lambda i: (i, 0)),
        out_shape=jax.ShapeDtypeStruct((m, n), F32),
        compiler_params=_params("parallel"),
        name=name,
    )(a, w, g.reshape(1, n), x)


def _ffn_kernel(x_ref, gpre_ref, wg_ref, wu_ref, wd_ref, gpost_ref, o_ref, hn_ref, act_ref):
    f = pl.program_id(1)
    nf = pl.num_programs(1) - 1

    def up_project():
        h = hn_ref[...]
        gate = _dot(h, wg_ref[...].astype(BF16))
        up = _dot(h, wu_ref[...].astype(BF16))
        act_ref[...] = (gate * _sigmoid(gate) * up).astype(BF16)

    def down_project():
        return _dot(act_ref[...], wd_ref[...].astype(BF16))

    @pl.when(f == 0)
    def _():
        hn_ref[...] = _rms(x_ref[...], gpre_ref[...]).astype(BF16)
        up_project()

    @pl.when(f == 1)
    def _():
        o_ref[...] = down_project()
        up_project()

    @pl.when((f > 1) & (f < nf))
    def _():
        o_ref[...] += down_project()
        up_project()

    @pl.when(f == nf)
    def _():
        o_ref[...] = x_ref[...] + _rms(o_ref[...] + down_project(), gpost_ref[...])


def _ffn(x, g_pre, w_gate_up, w_down, g_post, *, layer, tm=1024, tf=256, name):
    m, d = x.shape
    dff = w_down.shape[1]
    nf = dff // tf
    return pl.pallas_call(
        _ffn_kernel,
        grid=(m // tm, nf + 1),
        in_specs=[
            pl.BlockSpec((tm, d), lambda i, f: (i, 0)),
            pl.BlockSpec((1, d), lambda i, f: (0, 0)),
            pl.BlockSpec((None, d, tf), lambda i, f: (layer, 0, jnp.minimum(f, nf - 1))),
            pl.BlockSpec((None, d, tf), lambda i, f: (layer, 0, nf + jnp.minimum(f, nf - 1))),
            pl.BlockSpec((None, tf, d), lambda i, f: (layer, jnp.maximum(f - 1, 0), 0)),
            pl.BlockSpec((1, d), lambda i, f: (0, 0)),
        ],
        out_specs=pl.BlockSpec((tm, d), lambda i, f: (i, 0)),
        out_shape=jax.ShapeDtypeStruct((m, d), F32),
        scratch_shapes=[pltpu.VMEM((tm, d), BF16), pltpu.VMEM((tm, tf), BF16)],
        compiler_params=_params("parallel", "arbitrary"),
        name=name,
    )(x, g_pre.reshape(1, d), w_gate_up, w_gate_up, w_down, g_post.reshape(1, d))


def _t5_bias_tile(dist, table_ref, h):
    n = jnp.maximum(dist, 0)
    nf = jnp.maximum(n, 1).astype(F32)
    large = REL_MAX_EXACT + (jnp.log(nf / REL_MAX_EXACT) / math.log(REL_MAX_DIST / REL_MAX_EXACT)
                             * (REL_BUCKETS - REL_MAX_EXACT)).astype(jnp.int32)
    large = jnp.minimum(large, REL_BUCKETS - 1)
    bucket = jnp.where(n < REL_MAX_EXACT, n, large)
    far = table_ref[REL_BUCKETS - 1, h]
    bias = jnp.zeros(dist.shape, F32)
    for b in range(REL_BUCKETS - 1):
        bias = jnp.where(bucket == b, (table_ref[b, h] - far) * LOG2E, bias)
    return jnp.where(dist >= 0, bias, MASK_VALUE)


def _lane_chunks(xs):
    return [x[:, i:i + LANES] for x in xs for i in range(0, x.shape[1], LANES)]


def _attn_kernel(table_ref, lq1_ref, lk1_ref, lq2_ref, lk2_ref, q_ref, k_ref, v_ref, g_ref,
                 o_ref, bias_ref, *, lambda_init, tq):
    h = pl.program_id(0)
    b = pl.program_id(1)
    hd = DIFF_HD
    n_q = q_ref.shape[1] // tq

    @pl.when(b == 0)
    def _():
        rel = (lax.broadcasted_iota(jnp.int32, (tq, tq), 0)
               - lax.broadcasted_iota(jnp.int32, (tq, tq), 1))
        for r in range(2):
            bias_ref[r] = _t5_bias_tile(rel + r * tq, table_ref, h)

    lam = (jnp.exp(jnp.sum(lq1_ref[...] * lk1_ref[...], axis=-1, keepdims=True))
           - jnp.exp(jnp.sum(lq2_ref[...] * lk2_ref[...], axis=-1, keepdims=True))
           + lambda_init)
    g = g_ref[...]

    def segments(qi):
        segs = []
        if qi >= 2:
            segs.append((slice(0, (qi - 1) * tq), None))
        if qi >= 1:
            segs.append((slice((qi - 1) * tq, qi * tq), 1))
        segs.append((slice(qi * tq, (qi + 1) * tq), 0))
        return segs

    def scores(qi):
        q_rows = slice(qi * tq, (qi + 1) * tq)
        out = []
        for br in range(2):
            cols = slice(br * hd, (br + 1) * hd)
            qb = q_ref[0, q_rows, cols]
            s = []
            for k_rows, r in segments(qi):
                sc = _dot_nt(qb, k_ref[0, k_rows, cols])
                s.append(sc if r is None else sc + bias_ref[r])
            out.append(s)
        return out

    def weights(sc):
        probs = []
        for s in sc:
            m = jnp.max(functools.reduce(jnp.maximum, _lane_chunks(s)), axis=-1, keepdims=True)
            p = [jnp.exp2(x - m) for x in s]
            l = jnp.sum(functools.reduce(jnp.add, _lane_chunks(p)), axis=-1, keepdims=True)
            probs.append((p, l))
        c1 = 1.0 / probs[0][1]
        ratio = lam * probs[0][1] / probs[1][1]
        return [(p1 - p2 * ratio).astype(BF16) for p1, p2 in zip(probs[0][0], probs[1][0])], c1

    def output(qi, a, c1):
        o = None
        for (k_rows, _), ac in zip(segments(qi), a):
            part = _dot(ac, v_ref[0, k_rows, :])
            o = part if o is None else o + part
        o = o * c1
        o_ref[0, qi * tq:(qi + 1) * tq, :] = (_rms(o, g) * (1.0 - lambda_init)).astype(o_ref.dtype)

    sc_next = scores(0)
    pending = None
    for qi in range(n_q):
        sc = sc_next
        if qi + 1 < n_q:
            sc_next = scores(qi + 1)
        a = weights(sc)
        if pending is not None:
            output(*pending)
        pending = (qi,) + a
    output(*pending)


def _attention(table, lq1, lk1, lq2, lk2, qkv, g, *, lambda_init, tq=256):
    b, s, _ = qkv.shape
    hw = 2 * DIFF_HD
    k_off = D_MODEL // hw
    v_off = 2 * k_off
    vec = pl.BlockSpec((1, DIFF_HD), lambda h, i: (0, 0))
    return pl.pallas_call(
        functools.partial(_attn_kernel, lambda_init=lambda_init, tq=tq),
        grid=(DIFF_HEADS, b),
        in_specs=[
            pl.BlockSpec(memory_space=pltpu.SMEM),
            vec, vec, vec, vec,
            pl.BlockSpec((1, s, hw), lambda h, i: (i, 0, h)),
            pl.BlockSpec((1, s, hw), lambda h, i: (i, 0, k_off + h)),
            pl.BlockSpec((1, s, hw), lambda h, i: (i, 0, v_off + h)),
            pl.BlockSpec((1, hw), lambda h, i: (0, 0)),
        ],
        out_specs=pl.BlockSpec((1, s, hw), lambda h, i: (i, 0, h)),
        out_shape=jax.ShapeDtypeStruct((b, s, D_MODEL), BF16),
        scratch_shapes=[pltpu.VMEM((2, tq, tq), F32)],
        compiler_params=_params("arbitrary", "arbitrary"),
        name="diff_attention",
    )(table, lq1.reshape(1, -1), lk1.reshape(1, -1), lq2.reshape(1, -1), lk2.reshape(1, -1),
      qkv, qkv, qkv, g.reshape(1, hw))


def kernel(x, rel_bias_table, kv_norm_g, w_kv, gla_w_in, gla_w_fgate, gla_b_fgate, gla_norm_g, gla_w_out, diff_w_q, diff_lam_q1, diff_lam_k1, diff_lam_q2, diff_lam_k2, diff_subln_g, diff_w_out, pre_mix_g, post_mix_g, pre_ffn_g, post_ffn_g, ffn_w_gate_up, ffn_w_down):
    b, s, d = x.shape
    m = b * s
    xf = x.reshape(m, d)

    w_in = gla_w_in[0].astype(BF16)
    w_glr = jnp.pad(w_in[:, GLA_MAIN:], ((0, 0), (0, LANES - GATE_RANK)))
    proj, glr = _norm_matmul(xf, pre_mix_g[0], w_in, w_glr, n_out=GLA_MAIN, name="gla_in_proj")
    wf = jnp.pad(gla_w_fgate[0], ((0, LANES - GATE_RANK), (0, 0))).astype(BF16)
    o = _gla(proj.reshape(b, s, GLA_MAIN), glr.reshape(b, s, LANES), wf,
             gla_b_fgate[0].reshape(1, GLA_DK), gla_norm_g[0].reshape(1, GLA_HV))
    xf = _out_proj(o.reshape(m, d), gla_w_out[0], post_mix_g[0], xf, name="gla_out_proj")
    xf = _ffn(xf, pre_ffn_g[0], ffn_w_gate_up, ffn_w_down, post_ffn_g[0], layer=0, name="ffn0")

    i = N_A_LAYERS
    lambda_init = 0.8 - 0.6 * math.exp(-0.3 * i)
    w_qkv = jnp.concatenate([diff_w_q[0], w_kv], axis=1).astype(BF16)
    gains = jnp.stack([pre_mix_g[i], kv_norm_g])
    qkv = _norm_matmul(xf, gains, w_qkv, scale=DIFF_HD ** -0.5 * LOG2E, split=d,
                       name="diff_qkv_proj")
    o = _attention(rel_bias_table, diff_lam_q1[0], diff_lam_k1[0], diff_lam_q2[0], diff_lam_k2[0],
                   qkv.reshape(b, s, 3 * d), diff_subln_g[0], lambda_init=lambda_init)
    xf = _out_proj(o.reshape(m, d), diff_w_out[0], post_mix_g[i], xf, name="diff_out_proj")
    xf = _ffn(xf, pre_ffn_g[i], ffn_w_gate_up, ffn_w_down, post_ffn_g[i], layer=i, name="ffn1")
    return xf.reshape(b, s, d)
```

```python
import functools
import math

import jax
import jax.numpy as jnp
from jax import lax
from jax.experimental import pallas as pl
from jax.experimental.pallas import tpu as pltpu

F32 = jnp.float32
BF16 = jnp.bfloat16

D_MODEL = 2048
DEPTH = 2
N_A_LAYERS = DEPTH // 2
GLA_HEADS = 4
GLA_DK = D_MODEL // 2
GLA_DV = D_MODEL
GLA_HK = GLA_DK // GLA_HEADS
GLA_HV = GLA_DV // GLA_HEADS
GATE_RANK = 16
GATE_TAU = 16.0
GLA_CHUNK = 64
GLA_MAIN = 2 * GLA_DK + 2 * GLA_DV
DIFF_HEADS = 8
DIFF_HD = D_MODEL // DIFF_HEADS // 2
REL_BUCKETS = 32
REL_MAX_EXACT = REL_BUCKETS // 2
REL_MAX_DIST = 128
D_FF = ((8 * D_MODEL // 3 + 255) // 256) * 256
EPS = 1e-6

LANES = 128
LOG2E = math.log2(math.e)
MASK_VALUE = -1e30
VMEM_LIMIT = 60 * 1024 * 1024


def _rms(x, g):
    y = x * lax.rsqrt(jnp.mean(x * x, axis=-1, keepdims=True) + EPS)
    return y * g


def _sigmoid(x):
    return 1.0 / (1.0 + jnp.exp(-x))


def _dot(a, b):
    return jnp.dot(a, b, preferred_element_type=F32)


def _dot_nt(a, b):
    return lax.dot_general(a, b, (((1,), (1,)), ((), ())), preferred_element_type=F32)


def _dot_tn(a, b):
    return lax.dot_general(a, b, (((0,), (0,)), ((), ())), preferred_element_type=F32)


def _params(*sem):
    return pltpu.CompilerParams(dimension_semantics=sem, vmem_limit_bytes=VMEM_LIMIT)


def _norm_matmul_kernel(x_ref, g_ref, w_ref, *rest, scale, split, has_aux):
    if has_aux:
        wa_ref, o_ref, oa_ref, hn_ref = rest
    else:
        o_ref, hn_ref = rest
    j = pl.program_id(1)

    @pl.when((j == 0) | (j == split))
    def _():
        hn_ref[...] = _rms(x_ref[...], g_ref[...]).astype(BF16)

    if has_aux:
        @pl.when(j == 0)
        def _():
            oa_ref[...] = _dot(hn_ref[...], wa_ref[...]).astype(oa_ref.dtype)

    acc = _dot(hn_ref[...], w_ref[...])
    if scale != 1.0:
        acc = acc * jnp.where(j < split, scale, 1.0)
    o_ref[...] = acc.astype(o_ref.dtype)


def _norm_matmul(x, g, w, w_aux=None, *, scale=1.0, split=None, n_out=None, tm=1024, tn=2048,
                 name):
    m, k = x.shape
    n = w.shape[1] if n_out is None else n_out
    has_aux = w_aux is not None
    n_tiles = n // tn
    split = n_tiles if split is None else split // tn
    in_specs = [
        pl.BlockSpec((tm, k), lambda i, j: (i, 0)),
        pl.BlockSpec((None, 1, k), lambda i, j: (jnp.where(j < split, 0, 1), 0, 0)),
        pl.BlockSpec((k, tn), lambda i, j: (0, j)),
    ]
    out_shape = [jax.ShapeDtypeStruct((m, n), BF16)]
    out_specs = [pl.BlockSpec((tm, tn), lambda i, j: (i, j))]
    args = [x, g.reshape(-1, 1, k), w]
    if has_aux:
        na = w_aux.shape[1]
        in_specs.append(pl.BlockSpec((k, na), lambda i, j: (0, 0)))
        out_shape.append(jax.ShapeDtypeStruct((m, na), BF16))
        out_specs.append(pl.BlockSpec((tm, na), lambda i, j: (i, 0)))
        args.append(w_aux)
    res = pl.pallas_call(
        functools.partial(_norm_matmul_kernel, scale=scale, split=split, has_aux=has_aux),
        grid=(m // tm, n // tn),
        in_specs=in_specs,
        out_specs=out_specs,
        out_shape=out_shape,
        scratch_shapes=[pltpu.VMEM((tm, k), BF16)],
        compiler_params=_params("parallel", "arbitrary"),
        name=name,
    )(*args)
    return res if has_aux else res[0]


def _gla_kernel(q_ref, k_ref, v_ref, r_ref, glr_ref, wf_ref, bf_ref, gn_ref, o_ref, *, group):
    c_len = GLA_CHUNK
    t = group * c_len
    n_tiles = q_ref.shape[1] // t
    row = lax.broadcasted_iota(jnp.int32, (t, t), 0)
    col = lax.broadcasted_iota(jnp.int32, (t, t), 1)
    lag = row // c_len - col // c_len
    causal = (lag == 0) & (row >= col)
    prefix_mat = jnp.where(causal, 1.0, 0.0).astype(BF16)
    wf = wf_ref[...]
    bf = bf_ref[...]
    gn = gn_ref[...]

    def per_chunk_rows(vecs):
        return jnp.concatenate([jnp.broadcast_to(x, (c_len, x.shape[1])) for x in vecs], axis=0)

    def local_part(i):
        rows = slice(i * t, (i + 1) * t)
        q = q_ref[0, rows, :].astype(F32) * (GLA_HK ** -0.5)
        k = k_ref[0, rows, :].astype(F32)
        v = v_ref[0, rows, :]
        z = _dot(glr_ref[0, rows, :], wf) + bf
        log_a = (jnp.minimum(z, 0.0) - jnp.log1p(jnp.exp(-jnp.abs(z)))) / GATE_TAU
        la_hi = log_a.astype(BF16)
        la_lo = (log_a - la_hi.astype(F32)).astype(BF16)
        bcum = _dot(prefix_mat, la_hi) + _dot(prefix_mat, la_lo)
        tot = [bcum[(c + 1) * c_len - 1:(c + 1) * c_len] for c in range(group)]
        zero = jnp.zeros_like(tot[0])
        before = [zero]
        for c in range(1, group):
            before.append(before[-1] + tot[c - 1])
        tile_tot = before[-1] + tot[-1]
        after = [tile_tot - before[c] - tot[c] for c in range(group)]
        q_dec = q * jnp.exp(bcum)
        k_inv = (k * jnp.exp(-bcum)).astype(BF16)
        k_end = k * jnp.exp(per_chunk_rows(tot) - bcum)
        q_tile = (q_dec * per_chunk_rows([jnp.exp(x) for x in before])).astype(BF16)
        k_tile = (k_end * per_chunk_rows([jnp.exp(x) for x in after])).astype(BF16)
        k_end = k_end.astype(BF16)
        att = jnp.where(causal, _dot_nt(q_dec.astype(BF16), k_inv), 0.0)
        q_lags = []
        for l in range(1, group):
            between = [zero] * l + [before[c] - before[c - l + 1] for c in range(l, group)]
            q_l = q_dec if l == 1 else q_dec * per_chunk_rows([jnp.exp(x) for x in between])
            q_lags.append(q_l.astype(BF16))
        cross = _dot_nt(jnp.concatenate(q_lags, axis=0), k_end)
        for l in range(1, group):
            att = jnp.where(lag == l, cross[(l - 1) * t:l * t], att)
        o_local = _dot(att.astype(BF16), v)
        gate = r_ref[0, rows, :].astype(F32)
        gate = gate * _sigmoid(gate)
        return v, q_tile, k_tile, jnp.exp(tile_tot), o_local, gate

    def state_part(i, st, v, q_tile, k_tile, decay, o_local, gate):
        o = o_local + _dot_nt(q_tile, st.astype(BF16))
        o_ref[0, i * t:(i + 1) * t, :] = (_rms(o, gn) * gate).astype(o_ref.dtype)
        return st * decay + _dot_tn(v, k_tile)

    st = jnp.zeros((GLA_HV, GLA_HK), F32)
    nxt = local_part(0)
    for i in range(n_tiles):
        cur = nxt
        if i + 1 < n_tiles:
            nxt = local_part(i + 1)
        st = state_part(i, st, *cur)


def _gla(proj, glr, wf, bfg, gn, *, group=4):
    b, s, _ = proj.shape
    hk, hv = GLA_HK, GLA_HV
    k_off = GLA_DK // hk
    v_off = 2 * GLA_DK // hv
    r_off = (2 * GLA_DK + GLA_DV) // hv
    return pl.pallas_call(
        functools.partial(_gla_kernel, group=group),
        grid=(b, GLA_HEADS),
        in_specs=[
            pl.BlockSpec((1, s, hk), lambda i, h: (i, 0, h)),
            pl.BlockSpec((1, s, hk), lambda i, h: (i, 0, k_off + h)),
            pl.BlockSpec((1, s, hv), lambda i, h: (i, 0, v_off + h)),
            pl.BlockSpec((1, s, hv), lambda i, h: (i, 0, r_off + h)),
            pl.BlockSpec((1, s, LANES), lambda i, h: (i, 0, 0)),
            pl.BlockSpec((LANES, hk), lambda i, h: (0, h)),
            pl.BlockSpec((1, hk), lambda i, h: (0, h)),
            pl.BlockSpec((1, hv), lambda i, h: (0, 0)),
        ],
        out_specs=pl.BlockSpec((1, s, hv), lambda i, h: (i, 0, h)),
        out_shape=jax.ShapeDtypeStruct((b, s, GLA_DV), BF16),
        compiler_params=_params("parallel", "parallel"),
        name="gla_mixer",
    )(proj, proj, proj, proj, glr, wf, bfg, gn)


def _out_proj_kernel(a_ref, w_ref, g_ref, x_ref, o_ref):
    mix = _dot(a_ref[...], w_ref[...].astype(BF16))
    o_ref[...] = x_ref[...] + _rms(mix, g_ref[...])


def _out_proj(a, w, g, x, *, tm=512, name):
    m, k = a.shape
    n = w.shape[1]
    return pl.pallas_call(
        _out_proj_kernel,
        grid=(m // tm,),
        in_specs=[
            pl.BlockSpec((tm, k), lambda i: (i, 0)),
            pl.BlockSpec((k, n), lambda i: (0, 0), pipeline_mode=pl.Buffered(1)),
            pl.BlockSpec((1, n), lambda i: (0, 0)),
            pl.BlockSpec((tm, n), lambda i: (i, 0)),
        ],
        out_specs=pl.BlockSpec((tm, n), lambda i: (i, 0)),
        out_shape=jax.ShapeDtypeStruct((m, n), F32),
        compiler_params=_params("parallel"),
        name=name,
    )(a, w, g.reshape(1, n), x)


def _ffn_kernel(x_ref, gpre_ref, wg_ref, wu_ref, wd_ref, gpost_ref, o_ref, hn_ref):
    f = pl.program_id(1)

    @pl.when(f == 0)
    def _():
        hn_ref[...] = _rms(x_ref[...], gpre_ref[...]).astype(BF16)
        o_ref[...] = jnp.zeros_like(o_ref)

    h = hn_ref[...]
    gate = _dot(h, wg_ref[...].astype(BF16))
    up = _dot(h, wu_ref[...].astype(BF16))
    act = (gate * _sigmoid(gate) * up).astype(BF16)
    o_ref[...] += _dot(act, wd_ref[...].astype(BF16))

    @pl.when(f == pl.num_programs(1) - 1)
    def _():
        o_ref[...] = x_ref[...] + _rms(o_ref[...], gpost_ref[...])


def _ffn(x, g_pre, w_gate_up, w_down, g_post, *, layer, tm=1024, tf=256, name):
    m, d = x.shape
    dff = w_down.shape[1]
    nf = dff // tf
    return pl.pallas_call(
        _ffn_kernel,
        grid=(m // tm, nf),
        in_specs=[
            pl.BlockSpec((tm, d), lambda i, f: (i, 0)),
            pl.BlockSpec((1, d), lambda i, f: (0, 0)),
            pl.BlockSpec((None, d, tf), lambda i, f: (layer, 0, f)),
            pl.BlockSpec((None, d, tf), lambda i, f: (layer, 0, nf + f)),
            pl.BlockSpec((None, tf, d), lambda i, f: (layer, f, 0)),
            pl.BlockSpec((1, d), lambda i, f: (0, 0)),
        ],
        out_specs=pl.BlockSpec((tm, d), lambda i, f: (i, 0)),
        out_shape=jax.ShapeDtypeStruct((m, d), F32),
        scratch_shapes=[pltpu.VMEM((tm, d), BF16)],
        compiler_params=_params("parallel", "arbitrary"),
        name=name,
    )(x, g_pre.reshape(1, d), w_gate_up, w_gate_up, w_down, g_post.reshape(1, d))


def _t5_bias_tile(dist, table_ref, h):
    n = jnp.maximum(dist, 0)
    nf = jnp.maximum(n, 1).astype(F32)
    large = REL_MAX_EXACT + (jnp.log(nf / REL_MAX_EXACT) / math.log(REL_MAX_DIST / REL_MAX_EXACT)
                             * (REL_BUCKETS - REL_MAX_EXACT)).astype(jnp.int32)
    large = jnp.minimum(large, REL_BUCKETS - 1)
    bucket = jnp.where(n < REL_MAX_EXACT, n, large)
    far = table_ref[REL_BUCKETS - 1, h]
    bias = jnp.zeros(dist.shape, F32)
    for b in range(REL_BUCKETS - 1):
        bias = jnp.where(bucket == b, (table_ref[b, h] - far) * LOG2E, bias)
    return jnp.where(dist >= 0, bias, MASK_VALUE)


def _lane_chunks(xs):
    return [x[:, i:i + LANES] for x in xs for i in range(0, x.shape[1], LANES)]


def _attn_kernel(table_ref, lq1_ref, lk1_ref, lq2_ref, lk2_ref, q_ref, k_ref, v_ref, g_ref,
                 o_ref, bias_ref, *, lambda_init, tq):
    h = pl.program_id(0)
    b = pl.program_id(1)
    hd = DIFF_HD
    n_q = q_ref.shape[1] // tq

    @pl.when(b == 0)
    def _():
        rel = (lax.broadcasted_iota(jnp.int32, (tq, tq), 0)
               - lax.broadcasted_iota(jnp.int32, (tq, tq), 1))
        for r in range(2):
            bias_ref[r] = _t5_bias_tile(rel + r * tq, table_ref, h)

    lam = (jnp.exp(jnp.sum(lq1_ref[...] * lk1_ref[...], axis=-1, keepdims=True))
           - jnp.exp(jnp.sum(lq2_ref[...] * lk2_ref[...], axis=-1, keepdims=True))
           + lambda_init)
    g = g_ref[...]

    def segments(qi):
        segs = []
        if qi >= 2:
            segs.append((slice(0, (qi - 1) * tq), None))
        if qi >= 1:
            segs.append((slice((qi - 1) * tq, qi * tq), 1))
        segs.append((slice(qi * tq, (qi + 1) * tq), 0))
        return segs

    def scores(qi):
        q_rows = slice(qi * tq, (qi + 1) * tq)
        out = []
        for br in range(2):
            cols = slice(br * hd, (br + 1) * hd)
            qb = q_ref[0, q_rows, cols]
            s = []
            for k_rows, r in segments(qi):
                sc = _dot_nt(qb, k_ref[0, k_rows, cols])
                s.append(sc if r is None else sc + bias_ref[r])
            out.append(s)
        return out

    def weights(sc):
        probs = []
        for s in sc:
            m = jnp.max(functools.reduce(jnp.maximum, _lane_chunks(s)), axis=-1, keepdims=True)
            p = [jnp.exp2(x - m) for x in s]
            l = jnp.sum(functools.reduce(jnp.add, _lane_chunks(p)), axis=-1, keepdims=True)
            probs.append((p, l))
        c1 = 1.0 / probs[0][1]
        ratio = lam * probs[0][1] / probs[1][1]
        return [(p1 - p2 * ratio).astype(BF16) for p1, p2 in zip(probs[0][0], probs[1][0])], c1

    def output(qi, a, c1):
        o = None
        for (k_rows, _), ac in zip(segments(qi), a):
            part = _dot(ac, v_ref[0, k_rows, :])
            o = part if o is None else o + part
        o = o * c1
        o_ref[0, qi * tq:(qi + 1) * tq, :] = (_rms(o, g) * (1.0 - lambda_init)).astype(o_ref.dtype)

    sc_next = scores(0)
    pending = None
    for qi in range(n_q):
        sc = sc_next
        if qi + 1 < n_q:
            sc_next = scores(qi + 1)
        a = weights(sc)
        if pending is not None:
            output(*pending)
        pending = (qi,) + a
    output(*pending)


def _attention(table, lq1, lk1, lq2, lk2, qkv, g, *, lambda_init, tq=256):
    b, s, _ = qkv.shape
    hw = 2 * DIFF_HD
    k_off = D_MODEL // hw
    v_off = 2 * k_off
    vec = pl.BlockSpec((1, DIFF_HD), lambda h, i: (0, 0))
    return pl.pallas_call(
        functools.partial(_attn_kernel, lambda_init=lambda_init, tq=tq),
        grid=(DIFF_HEADS, b),
        in_specs=[
            pl.BlockSpec(memory_space=pltpu.SMEM),
            vec, vec, vec, vec,
            pl.BlockSpec((1, s, hw), lambda h, i: (i, 0, h)),
            pl.BlockSpec((1, s, hw), lambda h, i: (i, 0, k_off + h)),
            pl.BlockSpec((1, s, hw), lambda h, i: (i, 0, v_off + h)),
            pl.BlockSpec((1, hw), lambda h, i: (0, 0)),
        ],
        out_specs=pl.BlockSpec((1, s, hw), lambda h, i: (i, 0, h)),
        out_shape=jax.ShapeDtypeStruct((b, s, D_MODEL), BF16),
        scratch_shapes=[pltpu.VMEM((2, tq, tq), F32)],
        compiler_params=_params("arbitrary", "arbitrary"),
        name="diff_attention",
    )(table, lq1.reshape(1, -1), lk1.reshape(1, -1), lq2.reshape(1, -1), lk2.reshape(1, -1),
      qkv, qkv, qkv, g.reshape(1, hw))


def kernel(x, rel_bias_table, kv_norm_g, w_kv, gla_w_in, gla_w_fgate, gla_b_fgate, gla_norm_g, gla_w_out, diff_w_q, diff_lam_q1, diff_lam_k1, diff_lam_q2, diff_lam_k2, diff_subln_g, diff_w_out, pre_mix_g, post_mix_g, pre_ffn_g, post_ffn_g, ffn_w_gate_up, ffn_w_down):
    b, s, d = x.shape
    m = b * s
    xf = x.reshape(m, d)

    w_in = gla_w_in[0].astype(BF16)
    w_glr = jnp.pad(w_in[:, GLA_MAIN:], ((0, 0), (0, LANES - GATE_RANK)))
    proj, glr = _norm_matmul(xf, pre_mix_g[0], w_in, w_glr, n_out=GLA_MAIN, name="gla_in_proj")
    wf = jnp.pad(gla_w_fgate[0], ((0, LANES - GATE_RANK), (0, 0))).astype(BF16)
    o = _gla(proj.reshape(b, s, GLA_MAIN), glr.reshape(b, s, LANES), wf,
             gla_b_fgate[0].reshape(1, GLA_DK), gla_norm_g[0].reshape(1, GLA_HV))
    xf = _out_proj(o.reshape(m, d), gla_w_out[0], post_mix_g[0], xf, name="gla_out_proj")
    xf = _ffn(xf, pre_ffn_g[0], ffn_w_gate_up, ffn_w_down, post_ffn_g[0], layer=0, name="ffn0")

    i = N_A_LAYERS
    lambda_init = 0.8 - 0.6 * math.exp(-0.3 * i)
    w_qkv = jnp.concatenate([diff_w_q[0], w_kv], axis=1).astype(BF16)
    gains = jnp.stack([pre_mix_g[i], kv_norm_g])
    qkv = _norm_matmul(xf, gains, w_qkv, scale=DIFF_HD ** -0.5 * LOG2E, split=d,
                       name="diff_qkv_proj")
    o = _attention(rel_bias_table, diff_lam_q1[0], diff_lam_k1[0], diff_lam_q2[0], diff_lam_k2[0],
                   qkv.reshape(b, s, 3 * d), diff_subln_g[0], lambda_init=lambda_init)
    xf = _out_proj(o.reshape(m, d), diff_w_out[0], post_mix_g[i], xf, name="diff_out_proj")
    xf = _ffn(xf, pre_ffn_g[i], ffn_w_gate_up, ffn_w_down, post_ffn_g[i], layer=i, name="ffn1")
    return xf.reshape(b, s, d)
```

```python
import functools
import math

import jax
import jax.numpy as jnp
from jax import lax
from jax.experimental import pallas as pl
from jax.experimental.pallas import tpu as pltpu

F32 = jnp.float32
BF16 = jnp.bfloat16

D_MODEL = 2048
DEPTH = 2
N_A_LAYERS = DEPTH // 2
GLA_HEADS = 4
GLA_DK = D_MODEL // 2
GLA_DV = D_MODEL
GLA_HK = GLA_DK // GLA_HEADS
GLA_HV = GLA_DV // GLA_HEADS
GATE_RANK = 16
GATE_TAU = 16.0
GLA_CHUNK = 64
GLA_MAIN = 2 * GLA_DK + 2 * GLA_DV
DIFF_HEADS = 8
DIFF_HD = D_MODEL // DIFF_HEADS // 2
REL_BUCKETS = 32
REL_MAX_EXACT = REL_BUCKETS // 2
REL_MAX_DIST = 128
D_FF = ((8 * D_MODEL // 3 + 255) // 256) * 256
EPS = 1e-6

LANES = 128
LOG2E = math.log2(math.e)
MASK_VALUE = -1e30
VMEM_LIMIT = 63 * 1024 * 1024


def _rms(x, g):
    y = x * lax.rsqrt(jnp.mean(x * x, axis=-1, keepdims=True) + EPS)
    return y * g


def _sigmoid(x):
    return 1.0 / (1.0 + jnp.exp(-x))


def _dot(a, b):
    return jnp.dot(a, b, preferred_element_type=F32)


def _dot_nt(a, b):
    return lax.dot_general(a, b, (((1,), (1,)), ((), ())), preferred_element_type=F32)


def _dot_tn(a, b):
    return lax.dot_general(a, b, (((0,), (0,)), ((), ())), preferred_element_type=F32)


def _params(*sem):
    return pltpu.CompilerParams(dimension_semantics=sem, vmem_limit_bytes=VMEM_LIMIT)


def _norm_matmul_kernel(x_ref, g_ref, w_ref, *rest, scale, split, has_aux):
    if has_aux:
        wa_ref, o_ref, oa_ref, hn_ref = rest
    else:
        o_ref, hn_ref = rest
    j = pl.program_id(1)

    @pl.when((j == 0) | (j == split))
    def _():
        hn_ref[...] = _rms(x_ref[...], g_ref[...]).astype(BF16)

    if has_aux:
        @pl.when(j == 0)
        def _():
            oa_ref[...] = _dot(hn_ref[...], wa_ref[...]).astype(oa_ref.dtype)

    acc = _dot(hn_ref[...], w_ref[...])
    if scale != 1.0:
        acc = acc * jnp.where(j < split, scale, 1.0)
    o_ref[...] = acc.astype(o_ref.dtype)


def _norm_matmul(x, g, w, w_aux=None, *, scale=1.0, split=None, n_out=None, tm=1024, tn=2048,
                 name):
    m, k = x.shape
    n = w.shape[1] if n_out is None else n_out
    has_aux = w_aux is not None
    n_tiles = n // tn
    split = n_tiles if split is None else split // tn
    in_specs = [
        pl.BlockSpec((tm, k), lambda i, j: (i, 0)),
        pl.BlockSpec((None, 1, k), lambda i, j: (jnp.where(j < split, 0, 1), 0, 0)),
        pl.BlockSpec((k, tn), lambda i, j: (0, j)),
    ]
    out_shape = [jax.ShapeDtypeStruct((m, n), BF16)]
    out_specs = [pl.BlockSpec((tm, tn), lambda i, j: (i, j))]
    args = [x, g.reshape(-1, 1, k), w]
    if has_aux:
        na = w_aux.shape[1]
        in_specs.append(pl.BlockSpec((k, na), lambda i, j: (0, 0)))
        out_shape.append(jax.ShapeDtypeStruct((m, na), BF16))
        out_specs.append(pl.BlockSpec((tm, na), lambda i, j: (i, 0)))
        args.append(w_aux)
    res = pl.pallas_call(
        functools.partial(_norm_matmul_kernel, scale=scale, split=split, has_aux=has_aux),
        grid=(m // tm, n // tn),
        in_specs=in_specs,
        out_specs=out_specs,
        out_shape=out_shape,
        scratch_shapes=[pltpu.VMEM((tm, k), BF16)],
        compiler_params=_params("parallel", "arbitrary"),
        name=name,
    )(*args)
    return res if has_aux else res[0]


def _gla_kernel(q_ref, k_ref, v_ref, r_ref, glr_ref, wf_ref, bf_ref, gn_ref, o_ref, *, group):
    c_len = GLA_CHUNK
    t = group * c_len
    n_tiles = q_ref.shape[1] // t
    row = lax.broadcasted_iota(jnp.int32, (t, t), 0)
    col = lax.broadcasted_iota(jnp.int32, (t, t), 1)
    lag = row // c_len - col // c_len
    causal = (lag == 0) & (row >= col)
    prefix_mat = jnp.where(causal, 1.0, 0.0).astype(BF16)
    wf = wf_ref[...]
    bf = bf_ref[...]
    gn = gn_ref[...]

    def per_chunk_rows(vecs):
        return jnp.concatenate([jnp.broadcast_to(x, (c_len, x.shape[1])) for x in vecs], axis=0)

    def local_part(i):
        rows = slice(i * t, (i + 1) * t)
        q = q_ref[0, rows, :].astype(F32) * (GLA_HK ** -0.5)
        k = k_ref[0, rows, :].astype(F32)
        v = v_ref[0, rows, :]
        z = _dot(glr_ref[0, rows, :], wf) + bf
        log_a = (jnp.minimum(z, 0.0) - jnp.log1p(jnp.exp(-jnp.abs(z)))) / GATE_TAU
        la_hi = log_a.astype(BF16)
        la_lo = (log_a - la_hi.astype(F32)).astype(BF16)
        bcum = _dot(prefix_mat, la_hi) + _dot(prefix_mat, la_lo)
        tot = [bcum[(c + 1) * c_len - 1:(c + 1) * c_len] for c in range(group)]
        zero = jnp.zeros_like(tot[0])
        before = [zero]
        for c in range(1, group):
            before.append(before[-1] + tot[c - 1])
        tile_tot = before[-1] + tot[-1]
        after = [tile_tot - before[c] - tot[c] for c in range(group)]
        q_dec = q * jnp.exp(bcum)
        k_inv = (k * jnp.exp(-bcum)).astype(BF16)
        k_end = k * jnp.exp(per_chunk_rows(tot) - bcum)
        q_tile = (q_dec * per_chunk_rows([jnp.exp(x) for x in before])).astype(BF16)
        k_tile = (k_end * per_chunk_rows([jnp.exp(x) for x in after])).astype(BF16)
        k_end = k_end.astype(BF16)
        att = jnp.where(causal, _dot_nt(q_dec.astype(BF16), k_inv), 0.0)
        q_lags = []
        for l in range(1, group):
            between = [zero] * l + [before[c] - before[c - l + 1] for c in range(l, group)]
            q_l = q_dec if l == 1 else q_dec * per_chunk_rows([jnp.exp(x) for x in between])
            q_lags.append(q_l.astype(BF16))
        cross = _dot_nt(jnp.concatenate(q_lags, axis=0), k_end)
        for l in range(1, group):
            att = jnp.where(lag == l, cross[(l - 1) * t:l * t], att)
        o_local = _dot(att.astype(BF16), v)
        gate = r_ref[0, rows, :].astype(F32)
        gate = gate * _sigmoid(gate)
        return v, q_tile, k_tile, jnp.exp(tile_tot), o_local, gate

    def state_part(i, st, v, q_tile, k_tile, decay, o_local, gate):
        o = o_local + _dot_nt(q_tile, st.astype(BF16))
        o_ref[0, i * t:(i + 1) * t, :] = (_rms(o, gn) * gate).astype(o_ref.dtype)
        return st * decay + _dot_tn(v, k_tile)

    st = jnp.zeros((GLA_HV, GLA_HK), F32)
    nxt = local_part(0)
    for i in range(n_tiles):
        cur = nxt
        if i + 1 < n_tiles:
            nxt = local_part(i + 1)
        st = state_part(i, st, *cur)


def _gla(proj, glr, wf, bfg, gn, *, group=4):
    b, s, _ = proj.shape
    hk, hv = GLA_HK, GLA_HV
    k_off = GLA_DK // hk
    v_off = 2 * GLA_DK // hv
    r_off = (2 * GLA_DK + GLA_DV) // hv
    return pl.pallas_call(
        functools.partial(_gla_kernel, group=group),
        grid=(b, GLA_HEADS),
        in_specs=[
            pl.BlockSpec((1, s, hk), lambda i, h: (i, 0, h)),
            pl.BlockSpec((1, s, hk), lambda i, h: (i, 0, k_off + h)),
            pl.BlockSpec((1, s, hv), lambda i, h: (i, 0, v_off + h)),
            pl.BlockSpec((1, s, hv), lambda i, h: (i, 0, r_off + h)),
            pl.BlockSpec((1, s, LANES), lambda i, h: (i, 0, 0)),
            pl.BlockSpec((LANES, hk), lambda i, h: (0, h)),
            pl.BlockSpec((1, hk), lambda i, h: (0, h)),
            pl.BlockSpec((1, hv), lambda i, h: (0, 0)),
        ],
        out_specs=pl.BlockSpec((1, s, hv), lambda i, h: (i, 0, h)),
        out_shape=jax.ShapeDtypeStruct((b, s, GLA_DV), BF16),
        compiler_params=_params("parallel", "parallel"),
        name="gla_mixer",
    )(proj, proj, proj, proj, glr, wf, bfg, gn)


def _out_proj_kernel(a_ref, w_ref, g_ref, x_ref, o_ref):
    mix = _dot(a_ref[...], w_ref[...].astype(BF16))
    o_ref[...] = x_ref[...] + _rms(mix, g_ref[...])


def _out_proj(a, w, g, x, *, tm=512, name):
    m, k = a.shape
    n = w.shape[1]
    return pl.pallas_call(
        _out_proj_kernel,
        grid=(m // tm,),
        in_specs=[
            pl.BlockSpec((tm, k), lambda i: (i, 0)),
            pl.BlockSpec((k, n), lambda i: (0, 0), pipeline_mode=pl.Buffered(1)),
            pl.BlockSpec((1, n), lambda i: (0, 0)),
            pl.BlockSpec((tm, n), lambda i: (i, 0)),
        ],
        out_specs=pl.BlockSpec((tm, n), lambda i: (i, 0)),
        out_shape=jax.ShapeDtypeStruct((m, n), F32),
        compiler_params=_params("parallel"),
        name=name,
    )(a, w, g.reshape(1, n), x)


def _ffn_kernel(x_ref, gpre_ref, wg_ref, wu_ref, wd_ref, gpost_ref, o_ref, hn_ref):
    f = pl.program_id(1)

    @pl.when(f == 0)
    def _():
        hn_ref[...] = _rms(x_ref[...], gpre_ref[...]).astype(BF16)
        o_ref[...] = jnp.zeros_like(o_ref)

    h = hn_ref[...]
    gate = _dot(h, wg_ref[...])
    up = _dot(h, wu_ref[...])
    act = (gate * _sigmoid(gate) * up).astype(BF16)
    o_ref[...] += _dot(act, wd_ref[...])

    @pl.when(f == pl.num_programs(1) - 1)
    def _():
        o_ref[...] = x_ref[...] + _rms(o_ref[...], gpost_ref[...])


def _ffn(x, g_pre, w_gate_up, w_down, g_post, *, layer, tm=1024, tf=512, name):
    m, d = x.shape
    dff = w_down.shape[1]
    nf = dff // tf
    return pl.pallas_call(
        _ffn_kernel,
        grid=(m // tm, nf),
        in_specs=[
            pl.BlockSpec((tm, d), lambda i, f: (i, 0)),
            pl.BlockSpec((1, d), lambda i, f: (0, 0)),
            pl.BlockSpec((None, d, tf), lambda i, f: (layer, 0, f)),
            pl.BlockSpec((None, d, tf), lambda i, f: (layer, 0, nf + f)),
            pl.BlockSpec((None, tf, d), lambda i, f: (layer, f, 0)),
            pl.BlockSpec((1, d), lambda i, f: (0, 0)),
        ],
        out_specs=pl.BlockSpec((tm, d), lambda i, f: (i, 0)),
        out_shape=jax.ShapeDtypeStruct((m, d), F32),
        scratch_shapes=[pltpu.VMEM((tm, d), BF16)],
        compiler_params=_params("parallel", "arbitrary"),
        name=name,
    )(x, g_pre.reshape(1, d), w_gate_up, w_gate_up, w_down, g_post.reshape(1, d))


def _t5_bias_tile(dist, table_ref, h):
    n = jnp.maximum(dist, 0)
    nf = jnp.maximum(n, 1).astype(F32)
    large = REL_MAX_EXACT + (jnp.log(nf / REL_MAX_EXACT) / math.log(REL_MAX_DIST / REL_MAX_EXACT)
                             * (REL_BUCKETS - REL_MAX_EXACT)).astype(jnp.int32)
    large = jnp.minimum(large, REL_BUCKETS - 1)
    bucket = jnp.where(n < REL_MAX_EXACT, n, large)
    far = table_ref[REL_BUCKETS - 1, h]
    bias = jnp.zeros(dist.shape, F32)
    for b in range(REL_BUCKETS - 1):
        bias = jnp.where(bucket == b, (table_ref[b, h] - far) * LOG2E, bias)
    return jnp.where(dist >= 0, bias, MASK_VALUE)


def _lane_chunks(xs):
    return [x[:, i:i + LANES] for x in xs for i in range(0, x.shape[1], LANES)]


def _attn_kernel(table_ref, lq1_ref, lk1_ref, lq2_ref, lk2_ref, q_ref, k_ref, v_ref, g_ref,
                 o_ref, bias_ref, *, lambda_init, tq):
    h = pl.program_id(0)
    b = pl.program_id(1)
    hd = DIFF_HD
    n_q = q_ref.shape[1] // tq

    @pl.when(b == 0)
    def _():
        rel = (lax.broadcasted_iota(jnp.int32, (tq, tq), 0)
               - lax.broadcasted_iota(jnp.int32, (tq, tq), 1))
        for r in range(2):
            bias_ref[r] = _t5_bias_tile(rel + r * tq, table_ref, h)

    lam = (jnp.exp(jnp.sum(lq1_ref[...] * lk1_ref[...], axis=-1, keepdims=True))
           - jnp.exp(jnp.sum(lq2_ref[...] * lk2_ref[...], axis=-1, keepdims=True))
           + lambda_init)
    g = g_ref[...]

    def segments(qi):
        segs = []
        if qi >= 2:
            segs.append((slice(0, (qi - 1) * tq), None))
        if qi >= 1:
            segs.append((slice((qi - 1) * tq, qi * tq), 1))
        segs.append((slice(qi * tq, (qi + 1) * tq), 0))
        return segs

    def scores(qi):
        q_rows = slice(qi * tq, (qi + 1) * tq)
        out = []
        for br in range(2):
            cols = slice(br * hd, (br + 1) * hd)
            qb = q_ref[0, q_rows, cols]
            s = []
            for k_rows, r in segments(qi):
                sc = _dot_nt(qb, k_ref[0, k_rows, cols])
                s.append(sc if r is None else sc + bias_ref[r])
            out.append(s)
        return out

    def weights(sc):
        probs = []
        for s in sc:
            m = jnp.max(functools.reduce(jnp.maximum, _lane_chunks(s)), axis=-1, keepdims=True)
            p = [jnp.exp2(x - m) for x in s]
            l = jnp.sum(functools.reduce(jnp.add, _lane_chunks(p)), axis=-1, keepdims=True)
            probs.append((p, l))
        c1 = 1.0 / probs[0][1]
        ratio = lam * probs[0][1] / probs[1][1]
        return [(p1 - p2 * ratio).astype(BF16) for p1, p2 in zip(probs[0][0], probs[1][0])], c1

    def output(qi, a, c1):
        o = None
        for (k_rows, _), ac in zip(segments(qi), a):
            part = _dot(ac, v_ref[0, k_rows, :])
            o = part if o is None else o + part
        o = o * c1
        o_ref[0, qi * tq:(qi + 1) * tq, :] = (_rms(o, g) * (1.0 - lambda_init)).astype(o_ref.dtype)

    sc_next = scores(0)
    pending = None
    for qi in range(n_q):
        sc = sc_next
        if qi + 1 < n_q:
            sc_next = scores(qi + 1)
        a = weights(sc)
        if pending is not None:
            output(*pending)
        pending = (qi,) + a
    output(*pending)


def _attention(table, lq1, lk1, lq2, lk2, qkv, g, *, lambda_init, tq=128):
    b, s, _ = qkv.shape
    hw = 2 * DIFF_HD
    k_off = D_MODEL // hw
    v_off = 2 * k_off
    vec = pl.BlockSpec((1, DIFF_HD), lambda h, i: (0, 0))
    return pl.pallas_call(
        functools.partial(_attn_kernel, lambda_init=lambda_init, tq=tq),
        grid=(DIFF_HEADS, b),
        in_specs=[
            pl.BlockSpec(memory_space=pltpu.SMEM),
            vec, vec, vec, vec,
            pl.BlockSpec((1, s, hw), lambda h, i: (i, 0, h)),
            pl.BlockSpec((1, s, hw), lambda h, i: (i, 0, k_off + h)),
            pl.BlockSpec((1, s, hw), lambda h, i: (i, 0, v_off + h)),
            pl.BlockSpec((1, hw), lambda h, i: (0, 0)),
        ],
        out_specs=pl.BlockSpec((1, s, hw), lambda h, i: (i, 0, h)),
        out_shape=jax.ShapeDtypeStruct((b, s, D_MODEL), BF16),
        scratch_shapes=[pltpu.VMEM((2, tq, tq), F32)],
        compiler_params=_params("arbitrary", "arbitrary"),
        name="diff_attention",
    )(table, lq1.reshape(1, -1), lk1.reshape(1, -1), lq2.reshape(1, -1), lk2.reshape(1, -1),
      qkv, qkv, qkv, g.reshape(1, hw))


def kernel(x, rel_bias_table, kv_norm_g, w_kv, gla_w_in, gla_w_fgate, gla_b_fgate, gla_norm_g, gla_w_out, diff_w_q, diff_lam_q1, diff_lam_k1, diff_lam_q2, diff_lam_k2, diff_subln_g, diff_w_out, pre_mix_g, post_mix_g, pre_ffn_g, post_ffn_g, ffn_w_gate_up, ffn_w_down):
    b, s, d = x.shape
    m = b * s
    xf = x.reshape(m, d)

    w_in = gla_w_in[0].astype(BF16)
    w_glr = jnp.pad(w_in[:, GLA_MAIN:], ((0, 0), (0, LANES - GATE_RANK)))
    proj, glr = _norm_matmul(xf, pre_mix_g[0], w_in, w_glr, n_out=GLA_MAIN, name="gla_in_proj")
    wf = jnp.pad(gla_w_fgate[0], ((0, LANES - GATE_RANK), (0, 0))).astype(BF16)
    o = _gla(proj.reshape(b, s, GLA_MAIN), glr.reshape(b, s, LANES), wf,
             gla_b_fgate[0].reshape(1, GLA_DK), gla_norm_g[0].reshape(1, GLA_HV))
    xf = _out_proj(o.reshape(m, d), gla_w_out[0], post_mix_g[0], xf, name="gla_out_proj")
    w_gate_up = ffn_w_gate_up.astype(BF16)
    w_down = ffn_w_down.astype(BF16)
    xf = _ffn(xf, pre_ffn_g[0], w_gate_up, w_down, post_ffn_g[0], layer=0, name="ffn0")

    i = N_A_LAYERS
    lambda_init = 0.8 - 0.6 * math.exp(-0.3 * i)
    w_qkv = jnp.concatenate([diff_w_q[0], w_kv], axis=1).astype(BF16)
    gains = jnp.stack([pre_mix_g[i], kv_norm_g])
    qkv = _norm_matmul(xf, gains, w_qkv, scale=DIFF_HD ** -0.5 * LOG2E, split=d,
                       name="diff_qkv_proj")
    o = _attention(rel_bias_table, diff_lam_q1[0], diff_lam_k1[0], diff_lam_q2[0], diff_lam_k2[0],
                   qkv.reshape(b, s, 3 * d), diff_subln_g[0], lambda_init=lambda_init)
    xf = _out_proj(o.reshape(m, d), diff_w_out[0], post_mix_g[i], xf, name="diff_out_proj")
    xf = _ffn(xf, pre_ffn_g[i], w_gate_up, w_down, post_ffn_g[i], layer=i, name="ffn1")
    return xf.reshape(b, s, d)
```
